```python
import math
import jax, jax.numpy as jnp
from jax import lax
import numpy as np

D_MODEL = 1024
BATCH = 1
SEQ = 16384
DEPTH = 4

EPS = 1e-6
BLOCK = 128

MLA_HEADS = 8
MLA_Q_LORA = 256
MLA_KV_LORA = 128
MLA_NOPE = 64
MLA_ROPE = 32
MLA_V = 64
ROPE_BASE = 10000.0

DIL_HEADS = 8
DIL_HEAD_DIM = 64
DIL_PATTERNS = ((128, 1), (512, 4), (2048, 16))

SWA_Q_HEADS = 16
SWA_KV_HEADS = 2
SWA_HEAD_DIM = 64
SWA_WINDOW = 128

D_FF = -(-8 * D_MODEL // (3 * 256)) * 256

N_EVEN = (DEPTH + 1) // 2
N_ODD = DEPTH // 2

MIX_IN_A = MLA_Q_LORA + MLA_KV_LORA + MLA_ROPE
MIX_IN_B = 3 * DIL_HEADS * DIL_HEAD_DIM
MIX_IN = MIX_IN_A + MIX_IN_B
MIX_OUT = MLA_HEADS * MLA_V + DIL_HEADS * DIL_HEAD_DIM
MLA_QK = MLA_NOPE + MLA_ROPE
SWA_QKV = (SWA_Q_HEADS + 2 * SWA_KV_HEADS) * SWA_HEAD_DIM
SWA_OUT = SWA_Q_HEADS * SWA_HEAD_DIM

kernel_name = "hybrid_mla_dilated_swa_sink_trunk"


def _rmsnorm(x, g):
    xf = x.astype(jnp.float32)
    y = xf * lax.rsqrt(jnp.mean(xf * xf, axis=-1, keepdims=True) + EPS)
    return (y * g.astype(jnp.float32)).astype(x.dtype)


def _alibi_slopes(n):
    return jnp.exp2(-8.0 * jnp.arange(1, n + 1, dtype=jnp.float32) / n)


def _rope_tables(seq, dim):
    pos = jnp.arange(seq, dtype=jnp.float32)
    inv_freq = ROPE_BASE ** (-jnp.arange(0, dim, 2, dtype=jnp.float32) / dim)
    ang = pos[:, None] * inv_freq[None, :]
    return jnp.cos(ang), jnp.sin(ang)


def _rope(x, cos, sin):
    half = x.shape[-1] // 2
    x1, x2 = x[..., :half], x[..., half:]
    c = cos.astype(x.dtype)
    s = sin.astype(x.dtype)
    return jnp.concatenate([x1 * c - x2 * s, x2 * c + x1 * s], axis=-1)


def _to_blocks(t):
    b, s, h, d = t.shape
    return t.reshape(b, s // BLOCK, BLOCK, h, d).transpose(1, 0, 2, 3, 4)


def _from_blocks(t):
    nb, b, blk, h, d = t.shape
    return t.transpose(1, 0, 2, 3, 4).reshape(b, nb * blk, h * d)


def _mla_causal_attention(q, k, v):
    S = q.shape[1]
    nb = S // BLOCK
    scale = 1.0 / math.sqrt(q.shape[-1])
    kpos = jnp.arange(S)

    def one(args):
        qi, bi = args
        s = jnp.einsum('bqhd,bshd->bhqs', qi, k).astype(jnp.float32) * scale
        qpos = bi * BLOCK + jnp.arange(BLOCK)
        s = jnp.where((kpos[None, :] <= qpos[:, None])[None, None], s, -jnp.inf)
        p = jax.nn.softmax(s, axis=-1)
        return jnp.einsum('bhqs,bshd->bqhd', p.astype(v.dtype), v)

    out = lax.map(one, (_to_blocks(q), jnp.arange(nb)))
    return _from_blocks(out)


def _dilated_attention(q, k, v, slopes):
    S = q.shape[1]
    nb = S // BLOCK
    scale = 1.0 / math.sqrt(q.shape[-1])

    def one(args):
        qi, bi = args
        qpos = bi * BLOCK + jnp.arange(BLOCK)
        outs, lses = [], []
        for window, dil in DIL_PATTERNS:
            offs = dil * jnp.arange(window // dil + 1)
            kpos = qpos[:, None] - offs[None, :]
            valid = kpos >= 0
            idx = jnp.maximum(kpos, 0)
            kg = jnp.take(k, idx, axis=1)
            vg = jnp.take(v, idx, axis=1)
            s = jnp.einsum('bqhd,bqjhd->bhqj', qi, kg).astype(jnp.float32) * scale
            s = s - slopes[:, None, None] * offs.astype(jnp.float32)[None, None, :]
            s = jnp.where(valid[None, None], s, -jnp.inf)
            lse = jax.nn.logsumexp(s, axis=-1)
            p = jnp.exp(s - lse[..., None])
            outs.append(jnp.einsum('bhqj,bqjhd->bqhd', p.astype(v.dtype), vg))
            lses.append(lse)
        w = jax.nn.softmax(jnp.stack(lses, axis=0), axis=0)
        w = w.transpose(0, 1, 3, 2)[..., None].astype(v.dtype)
        return jnp.sum(w * jnp.stack(outs, axis=0), axis=0)

    out = lax.map(one, (_to_blocks(q), jnp.arange(nb)))
    return _from_blocks(out)


def _swa_sink_attention(q, k, v, sinks, slopes):
    B_, S, Hq, dh = q.shape
    Hkv = k.shape[2]
    G = Hq // Hkv
    nb = S // BLOCK
    scale = 1.0 / math.sqrt(dh)
    qb = q.reshape(B_, nb, BLOCK, Hkv, G, dh)

    def band(t):
        cur = t.reshape(B_, nb, BLOCK, Hkv, dh)
        prev = jnp.pad(t, ((0, 0), (BLOCK, 0), (0, 0), (0, 0)))[:, :S].reshape(B_, nb, BLOCK, Hkv, dh)
        return jnp.concatenate([prev, cur], axis=2)

    kb, vb = band(k), band(v)
    qi = jnp.arange(BLOCK)
    kj = jnp.arange(2 * BLOCK)
    dist = qi[:, None] + BLOCK - kj[None, :]
    kpos = jnp.arange(nb)[:, None] * BLOCK - BLOCK + kj[None, :]
    valid = ((dist >= 0) & (dist < SWA_WINDOW))[None] & (kpos >= 0)[:, None, :]
    s = jnp.einsum('bnqkgd,bnjkd->bnkgqj', qb, kb).astype(jnp.float32) * scale
    s = s - slopes.reshape(Hkv, G)[:, :, None, None] * dist.astype(jnp.float32)
    s = jnp.where(valid[None, :, None, None], s, -jnp.inf)
    sink = jnp.broadcast_to(sinks.astype(jnp.float32).reshape(1, 1, Hkv, G, 1, 1), s.shape[:-1] + (1,))
    p = jax.nn.softmax(jnp.concatenate([s, sink], axis=-1), axis=-1)[..., :-1]
    o = jnp.einsum('bnkgqj,bnjkd->bnqkgd', p.astype(v.dtype), vb)
    return o.reshape(B_, S, Hq * dh)


def _even_mixer(h, w_in, g_q, g_kv, w_uq, w_ukv, w_out, cos, sin, dil_slopes):
    B_, S, _ = h.shape
    proj = h @ w_in
    o1 = MLA_Q_LORA
    o2 = o1 + MLA_KV_LORA
    o3 = o2 + MLA_ROPE
    nd = DIL_HEADS * DIL_HEAD_DIM
    c_q, c_kv, k_r = proj[..., :o1], proj[..., o1:o2], proj[..., o2:o3]
    qB = proj[..., o3:o3 + nd].reshape(B_, S, DIL_HEADS, DIL_HEAD_DIM)
    kB = proj[..., o3 + nd:o3 + 2 * nd].reshape(B_, S, DIL_HEADS, DIL_HEAD_DIM)
    vB = proj[..., o3 + 2 * nd:o3 + 3 * nd].reshape(B_, S, DIL_HEADS, DIL_HEAD_DIM)
    qA = (_rmsnorm(c_q, g_q) @ w_uq).reshape(B_, S, MLA_HEADS, MLA_QK)
    q_nope, q_rope = qA[..., :MLA_NOPE], qA[..., MLA_NOPE:]
    q_rope = _rope(q_rope, cos[None, :, None, :], sin[None, :, None, :])
    kv = (_rmsnorm(c_kv, g_kv) @ w_ukv).reshape(B_, S, MLA_HEADS, MLA_NOPE + MLA_V)
    k_nope, vA = kv[..., :MLA_NOPE], kv[..., MLA_NOPE:]
    k_rope = _rope(k_r, cos[None], sin[None])
    k_rope = jnp.broadcast_to(k_rope[:, :, None, :], (B_, S, MLA_HEADS, MLA_ROPE))
    oA = _mla_causal_attention(jnp.concatenate([q_nope, q_rope], axis=-1),
                               jnp.concatenate([k_nope, k_rope], axis=-1), vA)
    oB = _dilated_attention(qB, kB, vB, dil_slopes)
    return jnp.concatenate([oA, oB], axis=-1) @ w_out


def _odd_mixer(h, w_qkv, sinks, w_out, swa_slopes):
    B_, S, _ = h.shape
    qkv = h @ w_qkv
    nq = SWA_Q_HEADS * SWA_HEAD_DIM
    nk = SWA_KV_HEADS * SWA_HEAD_DIM
    q = qkv[..., :nq].reshape(B_, S, SWA_Q_HEADS, SWA_HEAD_DIM)
    k = qkv[..., nq:nq + nk].reshape(B_, S, SWA_KV_HEADS, SWA_HEAD_DIM)
    v = qkv[..., nq + nk:].reshape(B_, S, SWA_KV_HEADS, SWA_HEAD_DIM)
    return _swa_sink_attention(q, k, v, sinks, swa_slopes) @ w_out


def _swiglu(h, w_gate_up, w_down):
    gu = h @ w_gate_up
    gate, up = gu[..., :D_FF], gu[..., D_FF:]
    return (jax.nn.silu(gate) * up) @ w_down


def setup_inputs(seed: int = 0) -> dict:
    key = jax.random.key(seed)
    ks = jax.random.split(key, 16)
    f32 = jnp.float32

    def w(k, shape, fan_in):
        return jax.random.normal(k, shape, f32) * (fan_in ** -0.5)

    def gain(k, shape):
        return 1.0 + 0.05 * jax.random.normal(k, shape, f32)

    return {
        "x": jax.random.normal(ks[0], (BATCH, SEQ, D_MODEL), f32),
        "attn_norm": gain(ks[1], (DEPTH, D_MODEL)),
        "ffn_norm": gain(ks[2], (DEPTH, D_MODEL)),
        "final_norm": gain(ks[3], (D_MODEL,)),
        "e_w_in": w(ks[4], (N_EVEN, D_MODEL, MIX_IN), D_MODEL),
        "e_q_norm": gain(ks[5], (N_EVEN, MLA_Q_LORA)),
        "e_kv_norm": gain(ks[6], (N_EVEN, MLA_KV_LORA)),
        "e_w_uq": w(ks[7], (N_EVEN, MLA_Q_LORA, MLA_HEADS * MLA_QK), MLA_Q_LORA),
        "e_w_ukv": w(ks[8], (N_EVEN, MLA_KV_LORA, MLA_HEADS * (MLA_NOPE + MLA_V)), MLA_KV_LORA),
        "e_w_out": w(ks[9], (N_EVEN, MIX_OUT, D_MODEL), MIX_OUT),
        "o_w_qkv": w(ks[10], (N_ODD, D_MODEL, SWA_QKV), D_MODEL),
        "o_sinks": jax.random.normal(ks[11], (N_ODD, SWA_Q_HEADS), f32),
        "o_w_out": w(ks[12], (N_ODD, SWA_OUT, D_MODEL), SWA_OUT),
        "f_w_gate_up": w(ks[13], (DEPTH, D_MODEL, 2 * D_FF), D_MODEL),
        "f_w_down": w(ks[14], (DEPTH, D_FF, D_MODEL), D_FF),
    }


def reference(x, attn_norm, ffn_norm, final_norm, e_w_in, e_q_norm, e_kv_norm, e_w_uq,
              e_w_ukv, e_w_out, o_w_qkv, o_sinks, o_w_out, f_w_gate_up, f_w_down):
    S = x.shape[1]
    cos, sin = _rope_tables(S, MLA_ROPE)
    dil_slopes = _alibi_slopes(DIL_HEADS)
    swa_slopes = _alibi_slopes(SWA_Q_HEADS)
    h = x
    for layer in range(DEPTH):
        i = layer // 2
        hn = _rmsnorm(h, attn_norm[layer])
        if layer % 2 == 0:
            mix = _even_mixer(hn, e_w_in[i], e_q_norm[i], e_kv_norm[i], e_w_uq[i],
                              e_w_ukv[i], e_w_out[i], cos, sin, dil_slopes)
        else:
            mix = _odd_mixer(hn, o_w_qkv[i], o_sinks[i], o_w_out[i], swa_slopes)
        h = h + mix
        h = h + _swiglu(_rmsnorm(h, ffn_norm[layer]), f_w_gate_up[layer], f_w_down[layer])
    return _rmsnorm(h, final_norm)
```

```python
import functools
import math

import jax
import jax.numpy as jnp
from jax import lax
from jax.experimental import pallas as pl
from jax.experimental.pallas import tpu as pltpu

F32 = jnp.float32
BF16 = jnp.bfloat16

D_MODEL = 1024
DEPTH = 4
EPS = 1e-6
MLA_HEADS = 8
MLA_Q_LORA = 256
MLA_KV_LORA = 128
MLA_NOPE = 64
MLA_ROPE = 32
MLA_V = 64
ROPE_BASE = 10000.0
DIL_HEADS = 8
DIL_HEAD_DIM = 64
DIL_PATTERNS = ((128, 1), (512, 4), (2048, 16))
SWA_Q_HEADS = 16
SWA_KV_HEADS = 2
SWA_HEAD_DIM = 64
SWA_WINDOW = 128
D_FF = 2816
MLA_QK = MLA_NOPE + MLA_ROPE

LANES = 128
V7X_VMEM_LIMIT = 56 * 1024 * 1024

BAND = 128
ROW_TILE = 256
POST_TILE = 512
NEG = -1e30
LOG2E = 1.4426950408889634


def _dot(a, b):
    return jnp.dot(a, b, preferred_element_type=F32)


def _dot_nt(a, b):
    return lax.dot_general(a, b, (((1,), (1,)), ((), ())), preferred_element_type=F32)


def _rms(x, g):
    return x * lax.rsqrt(jnp.mean(x * x, axis=-1, keepdims=True) + EPS) * g


def _resident(shape):
    return pl.BlockSpec(shape, lambda *_: (0,) * len(shape), pipeline_mode=pl.Buffered(1))


def _params(sem):
    return pltpu.CompilerParams(dimension_semantics=sem, vmem_limit_bytes=V7X_VMEM_LIMIT)


def _rope_group(x, ct, st, low):
    swapped = jnp.where(low, pltpu.roll(x, 112, 1), pltpu.roll(x, 16, 1))
    return x * ct + swapped * st


def _even_proj_kernel(h_ref, gn_ref, wa_ref, wb_ref, gq_ref, gkv_ref, wuq_ref, wuk_ref, wuvt_ref,
                      ct_ref, st_ref, q_ref, k_ref, vt_ref, qb_ref, kb_ref, vb_ref, *, q_scale, b_scale):
    hn = _rms(h_ref[...], gn_ref[...]).astype(BF16)
    pa = _dot(hn, wa_ref[...])
    pb = _dot(hn, wb_ref[...])
    nb = DIL_HEADS * DIL_HEAD_DIM
    qb_ref[...] = (pb[:, :nb] * b_scale).astype(BF16)
    kb_ref[...] = pb[:, nb:2 * nb].astype(BF16)
    vb_ref[...] = pb[:, 2 * nb:].astype(BF16)

    nq = _rms(pa[:, :MLA_Q_LORA], gq_ref[...]).astype(BF16)
    nkv = _rms(pa[:, MLA_Q_LORA:MLA_Q_LORA + MLA_KV_LORA], gkv_ref[...]).astype(BF16)
    ct = ct_ref[...]
    st = st_ref[...]
    low = lax.broadcasted_iota(jnp.int32, ct.shape, 1) < MLA_NOPE + MLA_ROPE // 2
    q = _dot(nq, wuq_ref[...])
    k = _dot(nkv, wuk_ref[...])
    k_rope = _rope_group(pa[:, MLA_Q_LORA + MLA_KV_LORA:], ct, st, low)
    for h in range(MLA_HEADS):
        sl = slice(LANES * h, LANES * (h + 1))
        q_ref[:, sl] = (_rope_group(q[:, sl], ct, st, low) * q_scale).astype(BF16)
        k_ref[:, sl] = (k[:, sl] + k_rope).astype(BF16)
    vt_ref[0] = _dot_nt(wuvt_ref[...], nkv).astype(BF16)


def _even_proj(h, gn, wa, wb, gq, gkv, wuq, wuk, wuvt, ct, st):
    S = h.shape[0]
    tm = ROW_TILE
    nb = DIL_HEADS * DIL_HEAD_DIM
    row = lambda w: pl.BlockSpec((tm, w), lambda i: (i, 0))
    kern = functools.partial(_even_proj_kernel, q_scale=LOG2E / math.sqrt(MLA_QK),
                             b_scale=1.0 / math.sqrt(DIL_HEAD_DIM))
    return pl.pallas_call(
        kern,
        grid=(S // tm,),
        in_specs=[row(D_MODEL), _resident(gn.shape), _resident(wa.shape), _resident(wb.shape),
                  _resident(gq.shape), _resident(gkv.shape), _resident(wuq.shape), _resident(wuk.shape),
                  _resident(wuvt.shape), row(LANES), row(LANES)],
        out_specs=[row(MLA_HEADS * LANES), row(MLA_HEADS * LANES),
                   pl.BlockSpec((1, MLA_HEADS * MLA_V, tm), lambda i: (i, 0, 0)),
                   row(nb), row(nb), row(nb)],
        out_shape=[jax.ShapeDtypeStruct((S, MLA_HEADS * LANES), BF16),
                   jax.ShapeDtypeStruct((S, MLA_HEADS * LANES), BF16),
                   jax.ShapeDtypeStruct((S // tm, MLA_HEADS * MLA_V, tm), BF16),
                   jax.ShapeDtypeStruct((S, nb), BF16),
                   jax.ShapeDtypeStruct((S, nb), BF16),
                   jax.ShapeDtypeStruct((S, nb), BF16)],
        compiler_params=_params(("parallel",)),
        name="even_proj",
    )(h, gn, wa, wb, gq, gkv, wuq, wuk, wuvt, ct, st)


def _mla_kernel(q_ref, k_ref, vt_ref, o_ref):
    tq = q_ref.shape[0]
    tk = vt_ref.shape[2]
    i = pl.program_id(1)
    row = lax.broadcasted_iota(jnp.int32, (tk, tq), 0)
    col = lax.broadcasted_iota(jnp.int32, (tk, tq), 1)
    causal = row <= col

    outs = []
    for hh in range(2):
        q = q_ref[:, LANES * hh:LANES * (hh + 1)]

        def chunk(j, carry, masked, hh=hh, q=q):
            m, l, acc = carry
            start = pl.multiple_of(j * tk, tk)
            kc = k_ref[pl.ds(start, tk), LANES * hh:LANES * (hh + 1)]
            s = _dot_nt(kc, q)
            if masked:
                s = jnp.where(causal, s, NEG)
            m_new = jnp.maximum(m, jnp.max(s, axis=0, keepdims=True))
            p = jnp.exp2(s - m_new)
            alpha = jnp.exp2(m - m_new)
            l = alpha * l + jnp.sum(p, axis=0, keepdims=True)
            vc = vt_ref[j, MLA_V * hh:MLA_V * (hh + 1), :]
            acc = alpha * acc + _dot(vc, p.astype(BF16))
            return m_new, l, acc

        init = (jnp.full((1, tq), NEG, F32), jnp.zeros((1, tq), F32), jnp.zeros((MLA_V, tq), F32))
        carry = lax.fori_loop(0, i, lambda j, c: chunk(j, c, False), init)
        m, l, acc = chunk(i, carry, True)
        outs.append(acc / l)
    o_ref[...] = jnp.concatenate(outs, axis=0).T.astype(BF16)


def _mla_attention(q, k, vt):
    S = q.shape[0]
    tq = ROW_TILE
    n_chunks, _, tk = vt.shape
    assert tk == tq and n_chunks * tk == S
    return pl.pallas_call(
        _mla_kernel,
        grid=(MLA_HEADS // 2, S // tq),
        in_specs=[pl.BlockSpec((tq, 2 * LANES), lambda hp, i: (i, hp)),
                  pl.BlockSpec((S, 2 * LANES), lambda hp, i: (0, hp)),
                  pl.BlockSpec((n_chunks, 2 * MLA_V, tk), lambda hp, i: (0, hp, 0))],
        out_specs=pl.BlockSpec((tq, 2 * MLA_V), lambda hp, i: (i, hp)),
        out_shape=jax.ShapeDtypeStruct((S, MLA_HEADS * MLA_V), BF16),
        compiler_params=_params(("parallel", "arbitrary")),
        name="mla_attention",
    )(q, k, vt)


def _band_kernel(*refs, n_pairs, q_pairs_per_kv, slopes, max_dist, init, final):
    refs = list(refs)
    sinks_ref = refs.pop(0) if init == "sink" else None
    q_ref, kp_ref, kc_ref, vp_ref, vc_ref = refs[:5]
    refs = refs[5:]
    if init == "state":
        m0_ref, l0_ref, a0_ref = refs[:3]
        refs = refs[3:]
    if final == "norm":
        (o_ref,) = refs
    else:
        m_ref, l_ref, a_ref = refs

    i = pl.program_id(1)
    tq = BAND
    qi = lax.broadcasted_iota(jnp.int32, (tq, 2 * tq), 0)
    kj = lax.broadcasted_iota(jnp.int32, (tq, 2 * tq), 1)
    dist = qi + tq - kj
    first_key = jnp.where(i > 0, 0, tq)
    valid = (dist >= 0) & (dist <= max_dist) & (kj >= first_key)
    distf = dist.astype(F32)
    kv_low = lax.broadcasted_iota(jnp.int32, (2 * tq, LANES), 1) < LANES // 2
    q_low = lax.broadcasted_iota(jnp.int32, (tq, LANES), 1) < LANES // 2
    head_lane = lax.broadcasted_iota(jnp.int32, (tq, LANES), 1)
    zero_kv = jnp.zeros((2 * tq, LANES), BF16)

    kv_cache = {}

    def kv_for(kvp):
        if kvp not in kv_cache:
            sl = slice(LANES * kvp, LANES * (kvp + 1))
            kb = jnp.concatenate([kp_ref[:, sl], kc_ref[:, sl]], axis=0)
            vb = jnp.concatenate([vp_ref[:, sl], vc_ref[:, sl]], axis=0)
            kv_cache[kvp] = ((jnp.where(kv_low, kb, zero_kv), jnp.where(kv_low, zero_kv, kb)),
                             (jnp.where(kv_low, vb, zero_kv), jnp.where(kv_low, zero_kv, vb)))
        return kv_cache[kvp]

    m_out = jnp.zeros((tq, LANES), F32)
    l_out = jnp.zeros((tq, LANES), F32)
    for p in range(n_pairs):
        sl = slice(LANES * p, LANES * (p + 1))
        qp = q_ref[:, sl]
        kz, vz = kv_for(p // q_pairs_per_kv)
        acc = None
        alphas, ls = [], []
        for e in range(2):
            h = 2 * p + e
            s = _dot_nt(qp, kz[e])
            s = s + jnp.where(valid, distf * (-slopes[h]), NEG)
            row_max = jnp.max(s, axis=1, keepdims=True)
            if init == "none":
                m_new = row_max
            else:
                if init == "sink":
                    m_prev = jnp.full((tq, 1), sinks_ref[h], F32)
                    l_prev = jnp.ones((tq, 1), F32)
                else:
                    m_prev = m0_ref[:, h:h + 1]
                    l_prev = l0_ref[:, h:h + 1]
                m_new = jnp.maximum(m_prev, row_max)
            pr = jnp.exp(s - m_new)
            l_new = jnp.sum(pr, axis=1, keepdims=True)
            if init != "none":
                alpha = jnp.exp(m_prev - m_new)
                l_new = l_new + alpha * l_prev
                alphas.append(alpha)
            ls.append(l_new)
            pv = _dot(pr.astype(BF16), vz[e])
            acc = pv if acc is None else acc + pv
            if final == "state":
                m_out = jnp.where(head_lane == h, m_new, m_out)
                l_out = jnp.where(head_lane == h, l_new, l_out)
        if init == "state":
            acc = acc + a0_ref[:, sl] * jnp.where(q_low, alphas[0], alphas[1])
        if final == "norm":
            o_ref[:, sl] = (acc * jnp.where(q_low, 1.0 / ls[0], 1.0 / ls[1])).astype(BF16)
        else:
            a_ref[:, sl] = acc
    if final == "state":
        m_ref[...] = m_out
        l_ref[...] = l_out


def _band_attention(q, k, v, *, dil, slopes, max_dist, init, final, sinks=None, state=None):
    S, wq = q.shape
    wk = k.shape[1]
    T = S // dil
    n_pairs = wq // LANES
    q_pairs_per_kv = n_pairs // (wk // LANES)
    view = lambda a: a.reshape(T, dil * a.shape[1])
    cur = lambda w: pl.BlockSpec((BAND, w), lambda r, i, *_: (i, r))
    prev = lambda w: pl.BlockSpec((BAND, w), lambda r, i, *_: (jnp.maximum(i - 1, 0), r))

    args, in_specs = [], []
    if init == "sink":
        args.append(sinks)
        in_specs.append(pl.BlockSpec(memory_space=pltpu.SMEM))
    args += [view(q), view(k), view(k), view(v), view(v)]
    in_specs += [cur(wq), prev(wk), cur(wk), prev(wk), cur(wk)]
    if init == "state":
        args += [view(a) for a in state]
        in_specs += [cur(LANES), cur(LANES), cur(wq)]
    if final == "norm":
        out_shape = jax.ShapeDtypeStruct((T, dil * wq), BF16)
        out_specs = cur(wq)
    else:
        out_shape = [jax.ShapeDtypeStruct((T, dil * LANES), F32), jax.ShapeDtypeStruct((T, dil * LANES), F32),
                     jax.ShapeDtypeStruct((T, dil * wq), F32)]
        out_specs = [cur(LANES), cur(LANES), cur(wq)]
    kern = functools.partial(_band_kernel, n_pairs=n_pairs, q_pairs_per_kv=q_pairs_per_kv,
                             slopes=tuple(float(s) * dil for s in slopes), max_dist=max_dist,
                             init=init, final=final)
    out = pl.pallas_call(
        kern,
        grid=(dil, T // BAND),
        in_specs=in_specs,
        out_specs=out_specs,
        out_shape=out_shape,
        compiler_params=_params(("parallel", "parallel")),
        name=f"band_d{dil}_{init}_{final}",
    )(*args)
    if final == "norm":
        return out.reshape(S, wq)
    return [o.reshape(S, -1) for o in out]


def _odd_proj_kernel(h_ref, gn_ref, wq_ref, wk_ref, wv_ref, q_ref, k_ref, v_ref, *, q_scale):
    hn = _rms(h_ref[...], gn_ref[...]).astype(BF16)
    q_ref[...] = (_dot(hn, wq_ref[...]) * q_scale).astype(BF16)
    k_ref[...] = _dot(hn, wk_ref[...]).astype(BF16)
    v_ref[...] = _dot(hn, wv_ref[...]).astype(BF16)


def _odd_proj(h, gn, wq, wk, wv):
    S = h.shape[0]
    tm = ROW_TILE
    row = lambda w: pl.BlockSpec((tm, w), lambda i: (i, 0))
    kern = functools.partial(_odd_proj_kernel, q_scale=1.0 / math.sqrt(SWA_HEAD_DIM))
    return pl.pallas_call(
        kern,
        grid=(S // tm,),
        in_specs=[row(D_MODEL), _resident(gn.shape), _resident(wq.shape), _resident(wk.shape),
                  _resident(wv.shape)],
        out_specs=[row(wq.shape[1]), row(wk.shape[1]), row(wv.shape[1])],
        out_shape=[jax.ShapeDtypeStruct((S, wq.shape[1]), BF16),
                   jax.ShapeDtypeStruct((S, wk.shape[1]), BF16),
                   jax.ShapeDtypeStruct((S, wv.shape[1]), BF16)],
        compiler_params=_params(("parallel",)),
        name="odd_proj",
    )(h, gn, wq, wk, wv)


def _post_kernel(*refs, final_norm):
    if final_norm:
        h_ref, a1_ref, a2_ref, wo1_ref, wo2_ref, gf_ref, wgu_ref, wd_ref, gfin_ref, o_ref = refs
    else:
        h_ref, a1_ref, a2_ref, wo1_ref, wo2_ref, gf_ref, wgu_ref, wd_ref, o_ref = refs
    h1 = h_ref[...] + _dot(a1_ref[...], wo1_ref[...]) + _dot(a2_ref[...], wo2_ref[...])
    hn = _rms(h1, gf_ref[...]).astype(BF16)
    gate = _dot(hn, wgu_ref[:, :D_FF])
    up = _dot(hn, wgu_ref[:, D_FF:])
    act = (gate / (1.0 + jnp.exp(-gate)) * up).astype(BF16)
    out = h1 + _dot(act, wd_ref[...])
    if final_norm:
        out = _rms(out, gfin_ref[...])
    o_ref[...] = out


def _post(h, a1, a2, wo1, wo2, gf, wgu, wd, gfin=None):
    S = h.shape[0]
    tm = POST_TILE
    row = lambda w, c=0: pl.BlockSpec((tm, w), lambda i, c=c: (i, c))
    args = [h, a1, a2, wo1, wo2, gf, wgu, wd]
    w_half = wo1.shape[0]
    in_specs = [row(D_MODEL), row(w_half, 0), row(w_half, 0 if a2 is not a1 else 1),
                _resident(wo1.shape), _resident(wo2.shape), _resident(gf.shape),
                _resident(wgu.shape), _resident(wd.shape)]
    if gfin is not None:
        args.append(gfin)
        in_specs.append(_resident(gfin.shape))
    return pl.pallas_call(
        functools.partial(_post_kernel, final_norm=gfin is not None),
        grid=(S // tm,),
        in_specs=in_specs,
        out_specs=row(D_MODEL),
        out_shape=jax.ShapeDtypeStruct((S, D_MODEL), F32),
        compiler_params=_params(("parallel",)),
        name="post_final" if gfin is not None else "post",
    )(*args)


def _alibi_slopes(n):
    return [2.0 ** (-8.0 * (i + 1) / n) for i in range(n)]


def _rope_tables(S):
    pos = jnp.arange(S, dtype=F32)
    inv_freq = ROPE_BASE ** (-jnp.arange(0, MLA_ROPE, 2, dtype=F32) / MLA_ROPE)
    ang = pos[:, None] * inv_freq[None, :]
    cos, sin = jnp.cos(ang), jnp.sin(ang)
    ones = jnp.ones((S, MLA_NOPE), F32)
    zeros = jnp.zeros((S, MLA_NOPE), F32)
    pad = jnp.zeros((S, LANES - MLA_QK), F32)
    return (jnp.concatenate([ones, cos, cos, pad], axis=1),
            jnp.concatenate([zeros, -sin, sin, pad], axis=1))


def _pad_heads(w, heads, width):
    rows = w.shape[0]
    w = w.reshape(rows, heads, width)
    return jnp.pad(w, ((0, 0), (0, 0), (0, LANES - width))).reshape(rows, heads * LANES)


def _even_weights(w_in, w_uq, w_ukv):
    o2 = MLA_Q_LORA + MLA_KV_LORA
    o3 = o2 + MLA_ROPE
    z = lambda n: jnp.zeros((D_MODEL, n), F32)
    wa = jnp.concatenate([w_in[:, :o2], z(MLA_NOPE), w_in[:, o2:o3], z(LANES - MLA_QK)], axis=1)
    wb = w_in[:, o3:]
    wuq = _pad_heads(w_uq, MLA_HEADS, MLA_QK)
    wukv = w_ukv.reshape(MLA_KV_LORA, MLA_HEADS, MLA_NOPE + MLA_V)
    wuk = _pad_heads(wukv[..., :MLA_NOPE].reshape(MLA_KV_LORA, -1), MLA_HEADS, MLA_NOPE)
    wuvt = wukv[..., MLA_NOPE:].reshape(MLA_KV_LORA, -1).T
    return [w.astype(BF16) for w in (wa, wb, wuq, wuk, wuvt)]


def _odd_weights(w_qkv):
    nq = SWA_Q_HEADS * SWA_HEAD_DIM
    nk = SWA_KV_HEADS * SWA_HEAD_DIM
    dup = lambda w: jnp.repeat(w.reshape(D_MODEL, SWA_KV_HEADS, 1, SWA_HEAD_DIM), 2, axis=2).reshape(D_MODEL, -1)
    return [w.astype(BF16) for w in (w_qkv[:, :nq], dup(w_qkv[:, nq:nq + nk]), dup(w_qkv[:, nq + nk:]))]


def kernel(x, attn_norm, ffn_norm, final_norm, e_w_in, e_q_norm, e_kv_norm, e_w_uq, e_w_ukv, e_w_out,
           o_w_qkv, o_sinks, o_w_out, f_w_gate_up, f_w_down):
    B, S, D = x.shape
    assert B == 1 and D == D_MODEL and S % POST_TILE == 0
    assert S % (BAND * max(d for _, d in DIL_PATTERNS)) == 0
    h = x.reshape(S, D)
    ct, st = _rope_tables(S)
    dil_slopes = _alibi_slopes(DIL_HEADS)
    swa_slopes = _alibi_slopes(SWA_Q_HEADS)
    row = lambda g: g.reshape(1, -1)

    for layer in range(DEPTH):
        i = layer // 2
        if layer % 2 == 0:
            wa, wb, wuq, wuk, wuvt = _even_weights(e_w_in[i], e_w_uq[i], e_w_ukv[i])
            q, k, vt, qb, kb, vb = _even_proj(h, row(attn_norm[layer]), wa, wb, row(e_q_norm[i]),
                                              row(e_kv_norm[i]), wuq, wuk, wuvt, ct, st)
            a1 = _mla_attention(q, k, vt)
            state = None
            for n, (window, dil) in enumerate(DIL_PATTERNS):
                last = n == len(DIL_PATTERNS) - 1
                out = _band_attention(qb, kb, vb, dil=dil, slopes=dil_slopes, max_dist=window // dil,
                                      init="none" if n == 0 else "state",
                                      final="norm" if last else "state", state=state)
                state = out
            a2 = out
            w_out = e_w_out[i].astype(BF16)
            half = MLA_HEADS * MLA_V
        else:
            wq, wk, wv = _odd_weights(o_w_qkv[i])
            q, k, v = _odd_proj(h, row(attn_norm[layer]), wq, wk, wv)
            a1 = a2 = _band_attention(q, k, v, dil=1, slopes=swa_slopes, max_dist=SWA_WINDOW - 1,
                                      init="sink", final="norm", sinks=o_sinks[i])
            w_out = o_w_out[i].astype(BF16)
            half = SWA_Q_HEADS * SWA_HEAD_DIM // 2
        h = _post(h, a1, a2, w_out[:half], w_out[half:], row(ffn_norm[layer]),
                  f_w_gate_up[layer].astype(BF16), f_w_down[layer].astype(BF16),
                  row(final_norm) if layer == DEPTH - 1 else None)
    return h.reshape(B, S, D)
```

```python
import functools
import math

import jax
import jax.numpy as jnp
from jax import lax
from jax.experimental import pallas as pl
from jax.experimental.pallas import tpu as pltpu

F32 = jnp.float32
BF16 = jnp.bfloat16

D_MODEL = 1024
DEPTH = 4
EPS = 1e-6
MLA_HEADS = 8
MLA_Q_LORA = 256
MLA_KV_LORA = 128
MLA_NOPE = 64
MLA_ROPE = 32
MLA_V = 64
ROPE_BASE = 10000.0
DIL_HEADS = 8
DIL_HEAD_DIM = 64
DIL_PATTERNS = ((128, 1), (512, 4), (2048, 16))
SWA_Q_HEADS = 16
SWA_KV_HEADS = 2
SWA_HEAD_DIM = 64
SWA_WINDOW = 128
D_FF = 2816
MLA_QK = MLA_NOPE + MLA_ROPE

LANES = 128
V7X_VMEM_LIMIT = 56 * 1024 * 1024

BAND = 128
ROW_TILE = 256
POST_TILE = 512
MLA_HEADS_PER_STEP = 2
NEG = -1e30
LOG2E = 1.4426950408889634


def _dot(a, b):
    return jnp.dot(a, b, preferred_element_type=F32)


def _dot_nt(a, b):
    return lax.dot_general(a, b, (((1,), (1,)), ((), ())), preferred_element_type=F32)


def _rms(x, g):
    return x * lax.rsqrt(jnp.mean(x * x, axis=-1, keepdims=True) + EPS) * g


def _resident(shape):
    return pl.BlockSpec(shape, lambda *_: (0,) * len(shape), pipeline_mode=pl.Buffered(1))


def _params(sem):
    return pltpu.CompilerParams(dimension_semantics=sem, vmem_limit_bytes=V7X_VMEM_LIMIT)


def _rope_group(x, ct, st, low):
    swapped = jnp.where(low, pltpu.roll(x, 112, 1), pltpu.roll(x, 16, 1))
    return x * ct + swapped * st


def _even_proj_kernel(h_ref, gn_ref, wa_ref, wb_ref, gq_ref, gkv_ref, wuq_ref, wuk_ref, wuvt_ref,
                      ct_ref, st_ref, q_ref, k_ref, vt_ref, qb_ref, kb_ref, vb_ref, *, q_scale, b_scale):
    hn = _rms(h_ref[...], gn_ref[...]).astype(BF16)
    pa = _dot(hn, wa_ref[...])
    pb = _dot(hn, wb_ref[...])
    nb = DIL_HEADS * DIL_HEAD_DIM
    qb_ref[...] = (pb[:, :nb] * b_scale).astype(BF16)
    kb_ref[...] = pb[:, nb:2 * nb].astype(BF16)
    vb_ref[...] = pb[:, 2 * nb:].astype(BF16)

    nq = _rms(pa[:, :MLA_Q_LORA], gq_ref[...]).astype(BF16)
    nkv = _rms(pa[:, MLA_Q_LORA:MLA_Q_LORA + MLA_KV_LORA], gkv_ref[...]).astype(BF16)
    ct = ct_ref[...]
    st = st_ref[...]
    low = lax.broadcasted_iota(jnp.int32, ct.shape, 1) < MLA_NOPE + MLA_ROPE // 2
    q = _dot(nq, wuq_ref[...])
    k = _dot(nkv, wuk_ref[...])
    k_rope = _rope_group(pa[:, MLA_Q_LORA + MLA_KV_LORA:], ct, st, low)
    for h in range(MLA_HEADS):
        sl = slice(LANES * h, LANES * (h + 1))
        q_ref[:, sl] = (_rope_group(q[:, sl], ct, st, low) * q_scale).astype(BF16)
        k_ref[:, sl] = (k[:, sl] + k_rope).astype(BF16)
    vt_ref[0] = _dot_nt(wuvt_ref[...], nkv).astype(BF16)


def _even_proj(h, gn, wa, wb, gq, gkv, wuq, wuk, wuvt, ct, st):
    S = h.shape[0]
    tm = ROW_TILE
    nb = DIL_HEADS * DIL_HEAD_DIM
    row = lambda w: pl.BlockSpec((tm, w), lambda i: (i, 0))
    kern = functools.partial(_even_proj_kernel, q_scale=LOG2E / math.sqrt(MLA_QK),
                             b_scale=1.0 / math.sqrt(DIL_HEAD_DIM))
    return pl.pallas_call(
        kern,
        grid=(S // tm,),
        in_specs=[row(D_MODEL), _resident(gn.shape), _resident(wa.shape), _resident(wb.shape),
                  _resident(gq.shape), _resident(gkv.shape), _resident(wuq.shape), _resident(wuk.shape),
                  _resident(wuvt.shape), row(LANES), row(LANES)],
        out_specs=[row(MLA_HEADS * LANES), row(MLA_HEADS * LANES),
                   pl.BlockSpec((1, MLA_HEADS * MLA_V, tm), lambda i: (i, 0, 0)),
                   row(nb), row(nb), row(nb)],
        out_shape=[jax.ShapeDtypeStruct((S, MLA_HEADS * LANES), BF16),
                   jax.ShapeDtypeStruct((S, MLA_HEADS * LANES), BF16),
                   jax.ShapeDtypeStruct((S // tm, MLA_HEADS * MLA_V, tm), BF16),
                   jax.ShapeDtypeStruct((S, nb), BF16),
                   jax.ShapeDtypeStruct((S, nb), BF16),
                   jax.ShapeDtypeStruct((S, nb), BF16)],
        compiler_params=_params(("parallel",)),
        name="even_proj",
    )(h, gn, wa, wb, gq, gkv, wuq, wuk, wuvt, ct, st)


def _mla_kernel(q_ref, k_ref, vt_ref, o_ref, s_ref, p_ref, acc_ref, *, n_heads):
    tq = q_ref.shape[0]
    tk = vt_ref.shape[2]
    assert tq == 2 * tk
    i = pl.program_id(1)
    heads = range(n_heads)
    lanes = lambda hh: slice(LANES * hh, LANES * (hh + 1))
    rows = lambda hh: slice(MLA_V * hh, MLA_V * (hh + 1))

    def scores(c, slot):
        start = pl.multiple_of(c * tk, tk)
        for hh in heads:
            s_ref[slot, hh] = _dot_nt(k_ref[pl.ds(start, tk), lanes(hh)], q_ref[:, lanes(hh)])

    def softmax(slot, ms, ls, mask):
        alphas, ms_new, ls_new = [], [], []
        for hh in heads:
            s = s_ref[slot, hh]
            if mask is not None:
                s = jnp.where(mask, s, NEG)
            m_new = jnp.maximum(ms[hh], jnp.max(s, axis=0, keepdims=True))
            p = jnp.exp2(s - m_new)
            alpha = jnp.exp2(ms[hh] - m_new)
            p_ref[slot, hh] = p.astype(BF16)
            alphas.append(alpha)
            ms_new.append(m_new)
            ls_new.append(alpha * ls[hh] + jnp.sum(p, axis=0, keepdims=True))
        return alphas, ms_new, ls_new

    def values(c, slot, alphas):
        for hh in heads:
            acc_ref[hh] = alphas[hh] * acc_ref[hh] + _dot(vt_ref[c, rows(hh), :], p_ref[slot, hh])

    def pair(t, carry):
        ms, ls, al_odd = carry
        c = 2 * t
        al_even, ms, ls = softmax(0, ms, ls, None)
        scores(c + 1, 1)
        values(jnp.maximum(c - 1, 0), 1, al_odd)
        al_odd, ms, ls = softmax(1, ms, ls, None)
        scores(c + 2, 0)
        values(c, 0, al_even)
        return ms, ls, al_odd

    acc_ref[...] = jnp.zeros(acc_ref.shape, F32)
    p_ref[1] = jnp.zeros(p_ref.shape[1:], BF16)
    scores(0, 0)
    init = ([jnp.full((1, tq), NEG, F32) for _ in heads], [jnp.zeros((1, tq), F32) for _ in heads],
            [jnp.ones((1, tq), F32) for _ in heads])
    ms, ls, al_odd = lax.fori_loop(0, i, pair, init)

    c = 2 * i
    row = lax.broadcasted_iota(jnp.int32, (tk, tq), 0)
    col = lax.broadcasted_iota(jnp.int32, (tk, tq), 1)
    values(jnp.maximum(c - 1, 0), 1, al_odd)
    scores(c + 1, 1)
    al_even, ms, ls = softmax(0, ms, ls, row <= col)
    values(c, 0, al_even)
    al_odd, ms, ls = softmax(1, ms, ls, row + tk <= col)
    values(c + 1, 1, al_odd)
    out_t = jnp.concatenate([acc_ref[hh] / ls[hh] for hh in heads], axis=0)
    o_ref[...] = out_t.T.astype(BF16)


def _mla_attention(q, k, vt):
    S = q.shape[0]
    nh = MLA_HEADS_PER_STEP
    n_chunks, _, tk = vt.shape
    tq = 2 * tk
    assert n_chunks * tk == S and S % tq == 0
    return pl.pallas_call(
        functools.partial(_mla_kernel, n_heads=nh),
        grid=(MLA_HEADS // nh, S // tq),
        in_specs=[pl.BlockSpec((tq, nh * LANES), lambda hp, i: (i, hp)),
                  pl.BlockSpec((S, nh * LANES), lambda hp, i: (0, hp)),
                  pl.BlockSpec((n_chunks, nh * MLA_V, tk), lambda hp, i: (0, hp, 0))],
        out_specs=pl.BlockSpec((tq, nh * MLA_V), lambda hp, i: (i, hp)),
        out_shape=jax.ShapeDtypeStruct((S, MLA_HEADS * MLA_V), BF16),
        scratch_shapes=[pltpu.VMEM((2, nh, tk, tq), F32), pltpu.VMEM((2, nh, tk, tq), BF16),
                        pltpu.VMEM((nh, MLA_V, tq), F32)],
        compiler_params=_params(("parallel", "arbitrary")),
        name="mla_attention",
    )(q, k, vt)


def _band_kernel(*refs, n_pairs, q_pairs_per_kv, slopes, max_dist, init, final):
    refs = list(refs)
    sinks_ref = refs.pop(0) if init == "sink" else None
    q_ref, kp_ref, kc_ref, vp_ref, vc_ref = refs[:5]
    refs = refs[5:]
    if init == "state":
        m0_ref, l0_ref, a0_ref = refs[:3]
        refs = refs[3:]
    if final == "norm":
        (o_ref,) = refs
    else:
        m_ref, l_ref, a_ref = refs

    i = pl.program_id(1)
    tq = BAND
    qi = lax.broadcasted_iota(jnp.int32, (tq, 2 * tq), 0)
    kj = lax.broadcasted_iota(jnp.int32, (tq, 2 * tq), 1)
    dist = qi + tq - kj
    first_key = jnp.where(i > 0, 0, tq)
    valid = (dist >= 0) & (dist <= max_dist) & (kj >= first_key)
    distf = dist.astype(F32)
    kv_low = lax.broadcasted_iota(jnp.int32, (2 * tq, LANES), 1) < LANES // 2
    q_low = lax.broadcasted_iota(jnp.int32, (tq, LANES), 1) < LANES // 2
    head_lane = lax.broadcasted_iota(jnp.int32, (tq, LANES), 1)
    zero_kv = jnp.zeros((2 * tq, LANES), BF16)

    kv_cache = {}

    def kv_for(kvp):
        if kvp not in kv_cache:
            sl = slice(LANES * kvp, LANES * (kvp + 1))
            kb = jnp.concatenate([kp_ref[:, sl], kc_ref[:, sl]], axis=0)
            vb = jnp.concatenate([vp_ref[:, sl], vc_ref[:, sl]], axis=0)
            kv_cache[kvp] = ((jnp.where(kv_low, kb, zero_kv), jnp.where(kv_low, zero_kv, kb)),
                             (jnp.where(kv_low, vb, zero_kv), jnp.where(kv_low, zero_kv, vb)))
        return kv_cache[kvp]

    m_out = jnp.zeros((tq, LANES), F32)
    l_out = jnp.zeros((tq, LANES), F32)
    for p in range(n_pairs):
        sl = slice(LANES * p, LANES * (p + 1))
        qp = q_ref[:, sl]
        kz, vz = kv_for(p // q_pairs_per_kv)
        acc = None
        alphas, ls = [], []
        for e in range(2):
            h = 2 * p + e
            s = _dot_nt(qp, kz[e])
            s = s + jnp.where(valid, distf * (-slopes[h]), NEG)
            row_max = jnp.max(s, axis=1, keepdims=True)
            if init == "none":
                m_new = row_max
            else:
                if init == "sink":
                    m_prev = jnp.full((tq, 1), sinks_ref[h], F32)
                    l_prev = jnp.ones((tq, 1), F32)
                else:
                    m_prev = m0_ref[:, h:h + 1]
                    l_prev = l0_ref[:, h:h + 1]
                m_new = jnp.maximum(m_prev, row_max)
            pr = jnp.exp(s - m_new)
            l_new = jnp.sum(pr, axis=1, keepdims=True)
            if init != "none":
                alpha = jnp.exp(m_prev - m_new)
                l_new = l_new + alpha * l_prev
                alphas.append(alpha)
            ls.append(l_new)
            pv = _dot(pr.astype(BF16), vz[e])
            acc = pv if acc is None else acc + pv
            if final == "state":
                m_out = jnp.where(head_lane == h, m_new, m_out)
                l_out = jnp.where(head_lane == h, l_new, l_out)
        if init == "state":
            acc = acc + a0_ref[:, sl] * jnp.where(q_low, alphas[0], alphas[1])
        if final == "norm":
            o_ref[:, sl] = (acc * jnp.where(q_low, 1.0 / ls[0], 1.0 / ls[1])).astype(BF16)
        else:
            a_ref[:, sl] = acc
    if final == "state":
        m_ref[...] = m_out
        l_ref[...] = l_out


def _band_attention(q, k, v, *, dil, slopes, max_dist, init, final, sinks=None, state=None):
    S, wq = q.shape
    wk = k.shape[1]
    T = S // dil
    n_pairs = wq // LANES
    q_pairs_per_kv = n_pairs // (wk // LANES)
    view = lambda a: a.reshape(T, dil * a.shape[1])
    cur = lambda w: pl.BlockSpec((BAND, w), lambda r, i, *_: (i, r))
    prev = lambda w: pl.BlockSpec((BAND, w), lambda r, i, *_: (jnp.maximum(i - 1, 0), r))

    args, in_specs = [], []
    if init == "sink":
        args.append(sinks)
        in_specs.append(pl.BlockSpec(memory_space=pltpu.SMEM))
    args += [view(q), view(k), view(k), view(v), view(v)]
    in_specs += [cur(wq), prev(wk), cur(wk), prev(wk), cur(wk)]
    if init == "state":
        args += [view(a) for a in state]
        in_specs += [cur(LANES), cur(LANES), cur(wq)]
    if final == "norm":
        out_shape = jax.ShapeDtypeStruct((T, dil * wq), BF16)
        out_specs = cur(wq)
    else:
        out_shape = [jax.ShapeDtypeStruct((T, dil * LANES), F32), jax.ShapeDtypeStruct((T, dil * LANES), F32),
                     jax.ShapeDtypeStruct((T, dil * wq), F32)]
        out_specs = [cur(LANES), cur(LANES), cur(wq)]
    kern = functools.partial(_band_kernel, n_pairs=n_pairs, q_pairs_per_kv=q_pairs_per_kv,
                             slopes=tuple(float(s) * dil for s in slopes), max_dist=max_dist,
                             init=init, final=final)
    out = pl.pallas_call(
        kern,
        grid=(dil, T // BAND),
        in_specs=in_specs,
        out_specs=out_specs,
        out_shape=out_shape,
        compiler_params=_params(("parallel", "parallel")),
        name=f"band_d{dil}_{init}_{final}",
    )(*args)
    if final == "norm":
        return out.reshape(S, wq)
    return [o.reshape(S, -1) for o in out]


def _odd_proj_kernel(h_ref, gn_ref, wq_ref, wk_ref, wv_ref, q_ref, k_ref, v_ref, *, q_scale):
    hn = _rms(h_ref[...], gn_ref[...]).astype(BF16)
    q_ref[...] = (_dot(hn, wq_ref[...]) * q_scale).astype(BF16)
    k_ref[...] = _dot(hn, wk_ref[...]).astype(BF16)
    v_ref[...] = _dot(hn, wv_ref[...]).astype(BF16)


def _odd_proj(h, gn, wq, wk, wv):
    S = h.shape[0]
    tm = ROW_TILE
    row = lambda w: pl.BlockSpec((tm, w), lambda i: (i, 0))
    kern = functools.partial(_odd_proj_kernel, q_scale=1.0 / math.sqrt(SWA_HEAD_DIM))
    return pl.pallas_call(
        kern,
        grid=(S // tm,),
        in_specs=[row(D_MODEL), _resident(gn.shape), _resident(wq.shape), _resident(wk.shape),
                  _resident(wv.shape)],
        out_specs=[row(wq.shape[1]), row(wk.shape[1]), row(wv.shape[1])],
        out_shape=[jax.ShapeDtypeStruct((S, wq.shape[1]), BF16),
                   jax.ShapeDtypeStruct((S, wk.shape[1]), BF16),
                   jax.ShapeDtypeStruct((S, wv.shape[1]), BF16)],
        compiler_params=_params(("parallel",)),
        name="odd_proj",
    )(h, gn, wq, wk, wv)


def _post_kernel(*refs, final_norm):
    if final_norm:
        h_ref, a1_ref, a2_ref, wo1_ref, wo2_ref, gf_ref, wgu_ref, wd_ref, gfin_ref, o_ref = refs
    else:
        h_ref, a1_ref, a2_ref, wo1_ref, wo2_ref, gf_ref, wgu_ref, wd_ref, o_ref = refs
    h1 = h_ref[...] + _dot(a1_ref[...], wo1_ref[...]) + _dot(a2_ref[...], wo2_ref[...])
    hn = _rms(h1, gf_ref[...]).astype(BF16)
    gate = _dot(hn, wgu_ref[:, :D_FF])
    up = _dot(hn, wgu_ref[:, D_FF:])
    act = (gate / (1.0 + jnp.exp(-gate)) * up).astype(BF16)
    out = h1 + _dot(act, wd_ref[...])
    if final_norm:
        out = _rms(out, gfin_ref[...])
    o_ref[...] = out


def _post(h, a1, a2, wo1, wo2, gf, wgu, wd, gfin=None):
    S = h.shape[0]
    tm = POST_TILE
    row = lambda w, c=0: pl.BlockSpec((tm, w), lambda i, c=c: (i, c))
    args = [h, a1, a2, wo1, wo2, gf, wgu, wd]
    w_half = wo1.shape[0]
    in_specs = [row(D_MODEL), row(w_half, 0), row(w_half, 0 if a2 is not a1 else 1),
                _resident(wo1.shape), _resident(wo2.shape), _resident(gf.shape),
                _resident(wgu.shape), _resident(wd.shape)]
    if gfin is not None:
        args.append(gfin)
        in_specs.append(_resident(gfin.shape))
    return pl.pallas_call(
        functools.partial(_post_kernel, final_norm=gfin is not None),
        grid=(S // tm,),
        in_specs=in_specs,
        out_specs=row(D_MODEL),
        out_shape=jax.ShapeDtypeStruct((S, D_MODEL), F32),
        compiler_params=_params(("parallel",)),
        name="post_final" if gfin is not None else "post",
    )(*args)


def _alibi_slopes(n):
    return [2.0 ** (-8.0 * (i + 1) / n) for i in range(n)]


def _rope_tables(S):
    pos = jnp.arange(S, dtype=F32)
    inv_freq = ROPE_BASE ** (-jnp.arange(0, MLA_ROPE, 2, dtype=F32) / MLA_ROPE)
    ang = pos[:, None] * inv_freq[None, :]
    cos, sin = jnp.cos(ang), jnp.sin(ang)
    ones = jnp.ones((S, MLA_NOPE), F32)
    zeros = jnp.zeros((S, MLA_NOPE), F32)
    pad = jnp.zeros((S, LANES - MLA_QK), F32)
    return (jnp.concatenate([ones, cos, cos, pad], axis=1),
            jnp.concatenate([zeros, -sin, sin, pad], axis=1))


def _pad_heads(w, heads, width):
    rows = w.shape[0]
    w = w.reshape(rows, heads, width)
    return jnp.pad(w, ((0, 0), (0, 0), (0, LANES - width))).reshape(rows, heads * LANES)


def _even_weights(w_in, w_uq, w_ukv):
    o2 = MLA_Q_LORA + MLA_KV_LORA
    o3 = o2 + MLA_ROPE
    z = lambda n: jnp.zeros((D_MODEL, n), F32)
    wa = jnp.concatenate([w_in[:, :o2], z(MLA_NOPE), w_in[:, o2:o3], z(LANES - MLA_QK)], axis=1)
    wb = w_in[:, o3:]
    wuq = _pad_heads(w_uq, MLA_HEADS, MLA_QK)
    wukv = w_ukv.reshape(MLA_KV_LORA, MLA_HEADS, MLA_NOPE + MLA_V)
    wuk = _pad_heads(wukv[..., :MLA_NOPE].reshape(MLA_KV_LORA, -1), MLA_HEADS, MLA_NOPE)
    wuvt = wukv[..., MLA_NOPE:].reshape(MLA_KV_LORA, -1).T
    return [w.astype(BF16) for w in (wa, wb, wuq, wuk, wuvt)]


def _odd_weights(w_qkv):
    nq = SWA_Q_HEADS * SWA_HEAD_DIM
    nk = SWA_KV_HEADS * SWA_HEAD_DIM
    dup = lambda w: jnp.repeat(w.reshape(D_MODEL, SWA_KV_HEADS, 1, SWA_HEAD_DIM), 2, axis=2).reshape(D_MODEL, -1)
    return [w.astype(BF16) for w in (w_qkv[:, :nq], dup(w_qkv[:, nq:nq + nk]), dup(w_qkv[:, nq + nk:]))]


def kernel(x, attn_norm, ffn_norm, final_norm, e_w_in, e_q_norm, e_kv_norm, e_w_uq, e_w_ukv, e_w_out,
           o_w_qkv, o_sinks, o_w_out, f_w_gate_up, f_w_down):
    B, S, D = x.shape
    assert B == 1 and D == D_MODEL and S % POST_TILE == 0
    assert S % (BAND * max(d for _, d in DIL_PATTERNS)) == 0
    h = x.reshape(S, D)
    ct, st = _rope_tables(S)
    dil_slopes = _alibi_slopes(DIL_HEADS)
    swa_slopes = _alibi_slopes(SWA_Q_HEADS)
    row = lambda g: g.reshape(1, -1)

    for layer in range(DEPTH):
        i = layer // 2
        if layer % 2 == 0:
            wa, wb, wuq, wuk, wuvt = _even_weights(e_w_in[i], e_w_uq[i], e_w_ukv[i])
            q, k, vt, qb, kb, vb = _even_proj(h, row(attn_norm[layer]), wa, wb, row(e_q_norm[i]),
                                              row(e_kv_norm[i]), wuq, wuk, wuvt, ct, st)
            a1 = _mla_attention(q, k, vt)
            state = None
            for n, (window, dil) in enumerate(DIL_PATTERNS):
                last = n == len(DIL_PATTERNS) - 1
                out = _band_attention(qb, kb, vb, dil=dil, slopes=dil_slopes, max_dist=window // dil,
                                      init="none" if n == 0 else "state",
                                      final="norm" if last else "state", state=state)
                state = out
            a2 = out
            w_out = e_w_out[i].astype(BF16)
            half = MLA_HEADS * MLA_V
        else:
            wq, wk, wv = _odd_weights(o_w_qkv[i])
            q, k, v = _odd_proj(h, row(attn_norm[layer]), wq, wk, wv)
            a1 = a2 = _band_attention(q, k, v, dil=1, slopes=swa_slopes, max_dist=SWA_WINDOW - 1,
                                      init="sink", final="norm", sinks=o_sinks[i])
            w_out = o_w_out[i].astype(BF16)
            half = SWA_Q_HEADS * SWA_HEAD_DIM // 2
        h = _post(h, a1, a2, w_out[:half], w_out[half:], row(ffn_norm[layer]),
                  f_w_gate_up[layer].astype(BF16), f_w_down[layer].astype(BF16),
                  row(final_norm) if layer == DEPTH - 1 else None)
    return h.reshape(B, S, D)
```

```python
import functools
import math

import jax
import jax.numpy as jnp
from jax import lax
from jax.experimental import pallas as pl
from jax.experimental.pallas import tpu as pltpu

F32 = jnp.float32
BF16 = jnp.bfloat16

D_MODEL = 1024
DEPTH = 4
EPS = 1e-6
MLA_HEADS = 8
MLA_Q_LORA = 256
MLA_KV_LORA = 128
MLA_NOPE = 64
MLA_ROPE = 32
MLA_V = 64
ROPE_BASE = 10000.0
DIL_HEADS = 8
DIL_HEAD_DIM = 64
DIL_PATTERNS = ((128, 1), (512, 4), (2048, 16))
SWA_Q_HEADS = 16
SWA_KV_HEADS = 2
SWA_HEAD_DIM = 64
SWA_WINDOW = 128
D_FF = 2816
MLA_QK = MLA_NOPE + MLA_ROPE
DIL_WIDTH = DIL_HEADS * DIL_HEAD_DIM

LANES = 128
V7X_VMEM_LIMIT = 56 * 1024 * 1024

BAND = 128
ROW_TILE = 256
POST_TILE = 512
MLA_HEADS_PER_STEP = 4
SUM_ROWS = 16
NEG = -1e30
LOG2E = 1.4426950408889634


def _dot(a, b):
    return jnp.dot(a, b, preferred_element_type=F32)


def _dot_nt(a, b):
    return lax.dot_general(a, b, (((1,), (1,)), ((), ())), preferred_element_type=F32)


def _rms(x, g):
    return x * lax.rsqrt(jnp.mean(x * x, axis=-1, keepdims=True) + EPS) * g


def _resident(shape):
    return pl.BlockSpec(shape, lambda *_: (0,) * len(shape), pipeline_mode=pl.Buffered(1))


def _params(sem):
    return pltpu.CompilerParams(dimension_semantics=sem, vmem_limit_bytes=V7X_VMEM_LIMIT)


def _rope_group(x, ct, st, low):
    swapped = jnp.where(low, pltpu.roll(x, 112, 1), pltpu.roll(x, 16, 1))
    return x * ct + swapped * st


def _even_proj_kernel(*refs, q_scale, b_scale, dils):
    (h_ref, gn_ref, wa_ref, wb_ref, gq_ref, gkv_ref, wuq_ref, wuk_ref, wuvt_ref, ct_ref, st_ref,
     q_ref, k_ref, vt_ref) = refs[:14]
    dil_refs = refs[14:14 + 3 * len(dils)]
    pb_ref = refs[-1]
    tm = h_ref.shape[0]
    nb = DIL_WIDTH

    hn = _rms(h_ref[...], gn_ref[...]).astype(BF16)
    pa = _dot(hn, wa_ref[...])
    pb = _dot(hn, wb_ref[...])
    groups = nb // LANES
    for g in range(3 * groups):
        cols = pb[:, LANES * g:LANES * (g + 1)]
        pb_ref[g] = cols * b_scale if g < groups else cols
    for n, d in enumerate(dils):
        for r in range(d):
            rows = pl.ds(r, tm // d, stride=d) if d > 1 else slice(None)
            for part in range(3):
                picked = [pb_ref[part * groups + g, rows, :] for g in range(groups)]
                dil_refs[3 * n + part][r] = jnp.concatenate(picked, axis=1).astype(BF16)

    nq = _rms(pa[:, :MLA_Q_LORA], gq_ref[...]).astype(BF16)
    nkv = _rms(pa[:, MLA_Q_LORA:MLA_Q_LORA + MLA_KV_LORA], gkv_ref[...]).astype(BF16)
    ct = ct_ref[...]
    st = st_ref[...]
    low = lax.broadcasted_iota(jnp.int32, ct.shape, 1) < MLA_NOPE + MLA_ROPE // 2
    q = _dot(nq, wuq_ref[...])
    k = _dot(nkv, wuk_ref[...])
    k_rope = _rope_group(pa[:, MLA_Q_LORA + MLA_KV_LORA:], ct, st, low)
    for h in range(MLA_HEADS):
        sl = slice(LANES * h, LANES * (h + 1))
        q_ref[:, sl] = (_rope_group(q[:, sl], ct, st, low) * q_scale).astype(BF16)
        k_ref[:, sl] = (k[:, sl] + k_rope).astype(BF16)
    vt_ref[0] = _dot_nt(wuvt_ref[...], nkv).astype(BF16)


def _even_proj(h, gn, wa, wb, gq, gkv, wuq, wuk, wuvt, ct, st, dils):
    S = h.shape[0]
    tm = ROW_TILE
    nb = DIL_WIDTH
    row = lambda w: pl.BlockSpec((tm, w), lambda i: (i, 0))
    kern = functools.partial(_even_proj_kernel, q_scale=LOG2E / math.sqrt(MLA_QK),
                             b_scale=1.0 / math.sqrt(DIL_HEAD_DIM), dils=dils)
    dil_specs, dil_shapes = [], []
    for d in dils:
        dil_specs += [pl.BlockSpec((d, tm // d, nb), lambda i: (0, i, 0))] * 3
        dil_shapes += [jax.ShapeDtypeStruct((d, S // d, nb), BF16)] * 3
    out = pl.pallas_call(
        kern,
        grid=(S // tm,),
        in_specs=[row(D_MODEL), _resident(gn.shape), _resident(wa.shape), _resident(wb.shape),
                  _resident(gq.shape), _resident(gkv.shape), _resident(wuq.shape), _resident(wuk.shape),
                  _resident(wuvt.shape), row(LANES), row(LANES)],
        out_specs=[row(MLA_HEADS * LANES), row(MLA_HEADS * LANES),
                   pl.BlockSpec((1, MLA_HEADS * MLA_V, tm), lambda i: (i, 0, 0))] + dil_specs,
        out_shape=[jax.ShapeDtypeStruct((S, MLA_HEADS * LANES), BF16),
                   jax.ShapeDtypeStruct((S, MLA_HEADS * LANES), BF16),
                   jax.ShapeDtypeStruct((S // tm, MLA_HEADS * MLA_V, tm), BF16)] + dil_shapes,
        scratch_shapes=[pltpu.VMEM((3 * nb // LANES, tm, LANES), F32)],
        compiler_params=_params(("parallel",)),
        name="even_proj",
    )(h, gn, wa, wb, gq, gkv, wuq, wuk, wuvt, ct, st)
    return out[:3], [out[3 + 3 * n:6 + 3 * n] for n in range(len(dils))]


def _mla_kernel(q_ref, k_ref, vt_ref, o_ref, s_ref, p_ref, acc_ref, *, n_heads):
    tq = q_ref.shape[0]
    tk = vt_ref.shape[2]
    assert tq == 2 * tk
    i = pl.program_id(1)
    heads = range(n_heads)
    lanes = lambda hh: slice(LANES * hh, LANES * (hh + 1))
    rows = lambda hh: slice(MLA_V * hh, MLA_V * (hh + 1))

    def scores(c, slot):
        start = pl.multiple_of(c * tk, tk)
        for hh in heads:
            s_ref[slot, hh] = _dot_nt(k_ref[pl.ds(start, tk), lanes(hh)], q_ref[:, lanes(hh)])

    def softmax(slot, ms, mask):
        alphas, ms_new = [], []
        for hh in heads:
            s = s_ref[slot, hh]
            if mask is not None:
                s = jnp.where(mask, s, NEG)
            m_new = jnp.maximum(ms[hh], jnp.max(s, axis=0, keepdims=True))
            p_ref[slot, hh] = jnp.exp2(s - m_new).astype(BF16)
            alphas.append(jnp.exp2(ms[hh] - m_new))
            ms_new.append(m_new)
        return alphas, ms_new

    ones = jnp.ones((SUM_ROWS, tk), BF16)

    def values(c, slot, alphas):
        for hh in heads:
            v_ones = jnp.concatenate([vt_ref[c, rows(hh), :], ones], axis=0)
            acc_ref[hh] = alphas[hh] * acc_ref[hh] + _dot(v_ones, p_ref[slot, hh])

    def pair(t, carry):
        ms, al_odd = carry
        c = 2 * t
        al_even, ms = softmax(0, ms, None)
        scores(c + 1, 1)
        values(jnp.maximum(c - 1, 0), 1, al_odd)
        al_odd, ms = softmax(1, ms, None)
        scores(c + 2, 0)
        values(c, 0, al_even)
        return ms, al_odd

    acc_ref[...] = jnp.zeros(acc_ref.shape, F32)
    p_ref[1] = jnp.zeros(p_ref.shape[1:], BF16)
    scores(0, 0)
    init = ([jnp.full((1, tq), NEG, F32) for _ in heads], [jnp.ones((1, tq), F32) for _ in heads])
    ms, al_odd = lax.fori_loop(0, i, pair, init)

    c = 2 * i
    row = lax.broadcasted_iota(jnp.int32, (tk, tq), 0)
    col = lax.broadcasted_iota(jnp.int32, (tk, tq), 1)
    values(jnp.maximum(c - 1, 0), 1, al_odd)
    scores(c + 1, 1)
    al_even, ms = softmax(0, ms, row <= col)
    values(c, 0, al_even)
    al_odd, ms = softmax(1, ms, row + tk <= col)
    values(c + 1, 1, al_odd)
    out_t = jnp.concatenate([acc_ref[hh, :MLA_V] / acc_ref[hh, MLA_V:MLA_V + 1] for hh in heads], axis=0)
    o_ref[...] = out_t.T.astype(BF16)


def _mla_attention(q, k, vt):
    S = q.shape[0]
    nh = MLA_HEADS_PER_STEP
    n_chunks, _, tk = vt.shape
    tq = 2 * tk
    assert n_chunks * tk == S and S % tq == 0
    return pl.pallas_call(
        functools.partial(_mla_kernel, n_heads=nh),
        grid=(MLA_HEADS // nh, S // tq),
        in_specs=[pl.BlockSpec((tq, nh * LANES), lambda hp, i: (i, hp)),
                  pl.BlockSpec((S, nh * LANES), lambda hp, i: (0, hp), pipeline_mode=pl.Buffered(1)),
                  pl.BlockSpec((n_chunks, nh * MLA_V, tk), lambda hp, i: (0, hp, 0),
                               pipeline_mode=pl.Buffered(1))],
        out_specs=pl.BlockSpec((tq, nh * MLA_V), lambda hp, i: (i, hp)),
        out_shape=jax.ShapeDtypeStruct((S, MLA_HEADS * MLA_V), BF16),
        scratch_shapes=[pltpu.VMEM((2, nh, tk, tq), F32), pltpu.VMEM((2, nh, tk, tq), BF16),
                        pltpu.VMEM((nh, MLA_V + SUM_ROWS, tq), F32)],
        compiler_params=_params(("parallel", "arbitrary")),
        name="mla_attention",
    )(q, k, vt)


def _band_kernel(*refs, n_pairs, q_pairs_per_kv, slopes, extra_key, use_sink, with_lse):
    refs = list(refs)
    sinks_ref = refs.pop(0) if use_sink else None
    q_ref, kp_ref, kc_ref, vp_ref, vc_ref = refs[:5]
    refs = refs[5:]
    o_ref = refs[0]
    lse_ref = refs[1] if with_lse else None

    i = pl.program_id(1)
    tq = BAND
    half = LANES // 2
    qi = lax.broadcasted_iota(jnp.int32, (tq, tq), 0)
    kj = lax.broadcasted_iota(jnp.int32, (tq, tq), 1)
    tri = kj <= qi
    diag = kj == qi
    distf = jnp.where(tri, qi - kj, qi - kj + tq).astype(F32)
    first_mask = jnp.where(i > 0, 0.0, jnp.where(tri, 0.0, NEG))
    kv_lane = lax.broadcasted_iota(jnp.int32, (2 * tq, LANES), 1)
    kv_low = kv_lane < half
    q_low = lax.broadcasted_iota(jnp.int32, (tq, LANES), 1) < half
    head_lane = lax.broadcasted_iota(jnp.int32, (tq, LANES), 1)
    zero_kv = jnp.zeros((2 * tq, LANES), BF16)
    rider = (jnp.where(kv_lane == half, 1.0, 0.0).astype(BF16), jnp.where(kv_lane == 0, 1.0, 0.0).astype(BF16))
    rider_lane = (half, 0)

    kv_cache = {}

    def kv_for(kvp):
        if kvp not in kv_cache:
            sl = slice(LANES * kvp, LANES * (kvp + 1))
            kb = jnp.concatenate([kp_ref[:, sl], kc_ref[:, sl]], axis=0)
            vb = jnp.concatenate([vp_ref[:, sl], vc_ref[:, sl]], axis=0)
            kv_cache[kvp] = ((jnp.where(kv_low, kb, zero_kv), jnp.where(kv_low, zero_kv, kb)),
                             (jnp.where(kv_low, vb, rider[0]), jnp.where(kv_low, rider[1], vb)))
        return kv_cache[kvp]

    n_heads = 2 * n_pairs
    scores = []
    for h in range(n_heads):
        kz, _ = kv_for((h // 2) // q_pairs_per_kv)
        scores.append(_dot_nt(q_ref[:, LANES * (h // 2):LANES * (h // 2 + 1)], kz[h % 2]))

    probs, maxes, fars = [], [], []
    for h in range(n_heads):
        s2 = scores[h]
        s = jnp.where(tri, s2[:, tq:], s2[:, :tq]) + (distf * (-slopes[h]) + first_mask)
        m = jnp.max(s, axis=1, keepdims=True)
        if extra_key:
            far_h = jnp.sum(jnp.where(diag, s2[:, :tq], 0.0), axis=1, keepdims=True) - slopes[h] * tq
            far_h = jnp.where(i > 0, far_h, NEG)
            m = jnp.maximum(m, far_h)
        if use_sink:
            m = jnp.maximum(m, sinks_ref[h])
        p = jnp.exp(s - m)
        zero = jnp.zeros_like(p)
        probs.append(jnp.concatenate([jnp.where(tri, zero, p), jnp.where(tri, p, zero)], axis=1).astype(BF16))
        maxes.append(m)
        fars.append(jnp.exp(far_h - m) if extra_key else None)

    lse = jnp.zeros((tq, LANES), F32)
    for pr in range(n_pairs):
        sl = slice(LANES * pr, LANES * (pr + 1))
        _, vz = kv_for(pr // q_pairs_per_kv)
        pv, ls = [], []
        for e in range(2):
            h = 2 * pr + e
            out = _dot(probs[h], vz[e])
            l = out[:, rider_lane[e]:rider_lane[e] + 1]
            if extra_key:
                l = l + fars[h]
            if use_sink:
                l = l + jnp.exp(sinks_ref[h] - maxes[h])
            pv.append(out)
            ls.append(l)
            if with_lse:
                lse = jnp.where(head_lane == h, maxes[h] + jnp.log(l), lse)
        acc = jnp.where(q_low, pv[0], pv[1])
        if extra_key:
            acc = acc + vp_ref[:, sl].astype(F32) * jnp.where(q_low, fars[2 * pr], fars[2 * pr + 1])
        o_ref[:, sl] = (acc * jnp.where(q_low, 1.0 / ls[0], 1.0 / ls[1])).astype(BF16)
    if with_lse:
        lse_ref[...] = lse


def _band_attention(q, k, v, *, slopes, max_dist, sinks=None, with_lse=False):
    d, T, wq = q.shape
    wk = k.shape[2]
    assert max_dist in (BAND - 1, BAND) and T % BAND == 0
    extra_key = max_dist == BAND
    n_pairs = wq // LANES
    q_pairs_per_kv = n_pairs // (wk // LANES)
    cur = lambda w: pl.BlockSpec((None, BAND, w), lambda r, i: (r, i, 0))
    prev = lambda w: pl.BlockSpec((None, BAND, w), lambda r, i: (r, jnp.maximum(i - 1, 0), 0))

    args, in_specs = [], []
    if sinks is not None:
        args.append(sinks)
        in_specs.append(pl.BlockSpec(memory_space=pltpu.SMEM))
    args += [q, k, k, v, v]
    in_specs += [cur(wq), prev(wk), cur(wk), prev(wk), cur(wk)]
    if extra_key:
        assert wq == wk
    out_shape = [jax.ShapeDtypeStruct((d, T, wq), BF16)]
    out_specs = [cur(wq)]
    if with_lse:
        out_shape.append(jax.ShapeDtypeStruct((d, T, LANES), F32))
        out_specs.append(cur(LANES))
    kern = functools.partial(_band_kernel, n_pairs=n_pairs, q_pairs_per_kv=q_pairs_per_kv,
                             slopes=tuple(float(s) for s in slopes), extra_key=extra_key,
                             use_sink=sinks is not None, with_lse=with_lse)
    out = pl.pallas_call(
        kern,
        grid=(d, T // BAND),
        in_specs=in_specs,
        out_specs=out_specs,
        out_shape=out_shape,
        compiler_params=_params(("parallel", "parallel")),
        name=f"band_d{d}" + ("_sink" if sinks is not None else ""),
    )(*args)
    return out if with_lse else out[0]


def _odd_proj_kernel(h_ref, gn_ref, wq_ref, wk_ref, wv_ref, q_ref, k_ref, v_ref, *, q_scale):
    hn = _rms(h_ref[...], gn_ref[...]).astype(BF16)
    q_ref[...] = (_dot(hn, wq_ref[...]) * q_scale).astype(BF16)
    k_ref[...] = _dot(hn, wk_ref[...]).astype(BF16)
    v_ref[...] = _dot(hn, wv_ref[...]).astype(BF16)


def _odd_proj(h, gn, wq, wk, wv):
    S = h.shape[0]
    tm = ROW_TILE
    row = lambda w: pl.BlockSpec((tm, w), lambda i: (i, 0))
    kern = functools.partial(_odd_proj_kernel, q_scale=1.0 / math.sqrt(SWA_HEAD_DIM))
    return pl.pallas_call(
        kern,
        grid=(S // tm,),
        in_specs=[row(D_MODEL), _resident(gn.shape), _resident(wq.shape), _resident(wk.shape),
                  _resident(wv.shape)],
        out_specs=[row(wq.shape[1]), row(wk.shape[1]), row(wv.shape[1])],
        out_shape=[jax.ShapeDtypeStruct((S, wq.shape[1]), BF16),
                   jax.ShapeDtypeStruct((S, wk.shape[1]), BF16),
                   jax.ShapeDtypeStruct((S, wv.shape[1]), BF16)],
        compiler_params=_params(("parallel",)),
        name="odd_proj",
    )(h, gn, wq, wk, wv)


def _merge_patterns(pat_refs, expand_ref, nat_refs, dils, tm):
    outs, lses = [], []
    nat = iter(nat_refs)
    for n, d in enumerate(dils):
        o_ref, l_ref = pat_refs[2 * n], pat_refs[2 * n + 1]
        if d == 1:
            outs.append(o_ref[0].astype(F32))
            lses.append(l_ref[0])
        else:
            o_nat, l_nat = next(nat), next(nat)
            for r in range(d):
                rows = pl.ds(r, tm // d, stride=d)
                block = o_ref[r].astype(F32)
                for g in range(o_nat.shape[0]):
                    o_nat[g, rows, :] = block[:, LANES * g:LANES * (g + 1)]
                l_nat[rows, :] = l_ref[r]
            outs.append(jnp.concatenate([o_nat[g] for g in range(o_nat.shape[0])], axis=1))
            lses.append(l_nat[...])
    top = functools.reduce(jnp.maximum, lses)
    es = [jnp.exp(l - top) for l in lses]
    inv = 1.0 / functools.reduce(jnp.add, es)
    merged = None
    for o, e in zip(outs, es):
        w = e * inv
        w_hi = w.astype(BF16)
        w_lo = (w - w_hi.astype(F32)).astype(BF16)
        w_full = _dot(w_hi, expand_ref[...]) + _dot(w_lo, expand_ref[...])
        merged = o * w_full if merged is None else merged + o * w_full
    return merged.astype(BF16)


def _post_kernel(*refs, final_norm, dils):
    refs = list(refs)
    h_ref, a1_ref = refs[:2]
    refs = refs[2:]
    tm = h_ref.shape[0]
    if dils:
        pat_refs, expand_ref = refs[:2 * len(dils)], refs[2 * len(dils)]
        refs = refs[2 * len(dils) + 1:]
    else:
        a2_ref = refs.pop(0)
    wo1_ref, wo2_ref, gf_ref, wgu_ref, wd_ref = refs[:5]
    refs = refs[5:]
    gfin_ref = refs.pop(0) if final_norm else None
    o_ref = refs.pop(0)
    a2 = _merge_patterns(pat_refs, expand_ref, refs, dils, tm) if dils else a2_ref[...]

    h1 = h_ref[...] + _dot(a1_ref[...], wo1_ref[...]) + _dot(a2, wo2_ref[...])
    hn = _rms(h1, gf_ref[...]).astype(BF16)
    gate = _dot(hn, wgu_ref[:, :D_FF])
    up = _dot(hn, wgu_ref[:, D_FF:])
    act = (gate / (1.0 + jnp.exp(-gate)) * up).astype(BF16)
    out = h1 + _dot(act, wd_ref[...])
    if final_norm:
        out = _rms(out, gfin_ref[...])
    o_ref[...] = out


def _post(h, a1, a2, wo1, wo2, gf, wgu, wd, gfin=None):
    S = h.shape[0]
    tm = POST_TILE
    w_half = wo1.shape[0]
    row = lambda w, c=0: pl.BlockSpec((tm, w), lambda i, c=c: (i, c))
    args, in_specs, scratch, dils = [h, a1], [row(D_MODEL), row(w_half, 0)], [], ()
    if isinstance(a2, list):
        dils = tuple(d for d, _, _ in a2)
        for d, o, lse in a2:
            args += [o, lse]
            in_specs += [pl.BlockSpec((d, tm // d, w_half), lambda i: (0, i, 0)),
                         pl.BlockSpec((d, tm // d, LANES), lambda i: (0, i, 0))]
            if d > 1:
                scratch += [pltpu.VMEM((w_half // LANES, tm, LANES), F32), pltpu.VMEM((tm, LANES), F32)]
        head_of_lane = jnp.arange(w_half) // DIL_HEAD_DIM
        expand = (jnp.arange(LANES)[:, None] == head_of_lane[None, :]).astype(BF16)
        args.append(expand)
        in_specs.append(_resident(expand.shape))
    else:
        args.append(a2)
        in_specs.append(row(w_half, 1 if a2 is a1 else 0))
    args += [wo1, wo2, gf, wgu, wd]
    in_specs += [_resident(wo1.shape), _resident(wo2.shape), _resident(gf.shape),
                 _resident(wgu.shape), _resident(wd.shape)]
    if gfin is not None:
        args.append(gfin)
        in_specs.append(_resident(gfin.shape))
    return pl.pallas_call(
        functools.partial(_post_kernel, final_norm=gfin is not None, dils=dils),
        grid=(S // tm,),
        in_specs=in_specs,
        out_specs=row(D_MODEL),
        out_shape=jax.ShapeDtypeStruct((S, D_MODEL), F32),
        scratch_shapes=scratch,
        compiler_params=_params(("parallel",)),
        name=("post_merge" if dils else "post") + ("_final" if gfin is not None else ""),
    )(*args)


def _alibi_slopes(n):
    return [2.0 ** (-8.0 * (i + 1) / n) for i in range(n)]


def _rope_tables(S):
    pos = jnp.arange(S, dtype=F32)
    inv_freq = ROPE_BASE ** (-jnp.arange(0, MLA_ROPE, 2, dtype=F32) / MLA_ROPE)
    ang = pos[:, None] * inv_freq[None, :]
    cos, sin = jnp.cos(ang), jnp.sin(ang)
    ones = jnp.ones((S, MLA_NOPE), F32)
    zeros = jnp.zeros((S, MLA_NOPE), F32)
    pad = jnp.zeros((S, LANES - MLA_QK), F32)
    return (jnp.concatenate([ones, cos, cos, pad], axis=1),
            jnp.concatenate([zeros, -sin, sin, pad], axis=1))


def _pad_heads(w, heads, width):
    rows = w.shape[0]
    w = w.reshape(rows, heads, width)
    return jnp.pad(w, ((0, 0), (0, 0), (0, LANES - width))).reshape(rows, heads * LANES)


def _even_weights(w_in, w_uq, w_ukv):
    o2 = MLA_Q_LORA + MLA_KV_LORA
    o3 = o2 + MLA_ROPE
    z = lambda n: jnp.zeros((D_MODEL, n), F32)
    wa = jnp.concatenate([w_in[:, :o2], z(MLA_NOPE), w_in[:, o2:o3], z(LANES - MLA_QK)], axis=1)
    wb = w_in[:, o3:]
    wuq = _pad_heads(w_uq, MLA_HEADS, MLA_QK)
    wukv = w_ukv.reshape(MLA_KV_LORA, MLA_HEADS, MLA_NOPE + MLA_V)
    wuk = _pad_heads(wukv[..., :MLA_NOPE].reshape(MLA_KV_LORA, -1), MLA_HEADS, MLA_NOPE)
    wuvt = wukv[..., MLA_NOPE:].reshape(MLA_KV_LORA, -1).T
    return [w.astype(BF16) for w in (wa, wb, wuq, wuk, wuvt)]


def _odd_weights(w_qkv):
    nq = SWA_Q_HEADS * SWA_HEAD_DIM
    nk = SWA_KV_HEADS * SWA_HEAD_DIM
    dup = lambda w: jnp.repeat(w.reshape(D_MODEL, SWA_KV_HEADS, 1, SWA_HEAD_DIM), 2, axis=2).reshape(D_MODEL, -1)
    return [w.astype(BF16) for w in (w_qkv[:, :nq], dup(w_qkv[:, nq:nq + nk]), dup(w_qkv[:, nq + nk:]))]


def kernel(x, attn_norm, ffn_norm, final_norm, e_w_in, e_q_norm, e_kv_norm, e_w_uq, e_w_ukv, e_w_out,
           o_w_qkv, o_sinks, o_w_out, f_w_gate_up, f_w_down):
    B, S, D = x.shape
    dils = tuple(d for _, d in DIL_PATTERNS)
    assert B == 1 and D == D_MODEL and S % POST_TILE == 0 and S % (BAND * max(dils)) == 0
    h = x.reshape(S, D)
    ct, st = _rope_tables(S)
    dil_slopes = _alibi_slopes(DIL_HEADS)
    swa_slopes = _alibi_slopes(SWA_Q_HEADS)
    row = lambda g: g.reshape(1, -1)

    for layer in range(DEPTH):
        i = layer // 2
        if layer % 2 == 0:
            wa, wb, wuq, wuk, wuvt = _even_weights(e_w_in[i], e_w_uq[i], e_w_ukv[i])
            (q, k, vt), dilated = _even_proj(h, row(attn_norm[layer]), wa, wb, row(e_q_norm[i]),
                                             row(e_kv_norm[i]), wuq, wuk, wuvt, ct, st, dils)
            a1 = _mla_attention(q, k, vt)
            a2 = []
            for (window, d), (qd, kd, vd) in zip(DIL_PATTERNS, dilated):
                o, lse = _band_attention(qd, kd, vd, slopes=[s * d for s in dil_slopes],
                                         max_dist=window // d, with_lse=True)
                a2.append((d, o, lse))
            w_out = e_w_out[i].astype(BF16)
            half = MLA_HEADS * MLA_V
        else:
            wq, wk, wv = _odd_weights(o_w_qkv[i])
            q, k, v = _odd_proj(h, row(attn_norm[layer]), wq, wk, wv)
            a1 = a2 = _band_attention(q[None], k[None], v[None], slopes=swa_slopes,
                                      max_dist=SWA_WINDOW - 1, sinks=o_sinks[i])[0]
            w_out = o_w_out[i].astype(BF16)
            half = SWA_Q_HEADS * SWA_HEAD_DIM // 2
        h = _post(h, a1, a2, w_out[:half], w_out[half:], row(ffn_norm[layer]),
                  f_w_gate_up[layer].astype(BF16), f_w_down[layer].astype(BF16),
                  row(final_norm) if layer == DEPTH - 1 else None)
    return h.reshape(B, S, D)
```

```python
import functools
import math

import jax
import jax.numpy as jnp
from jax import lax
from jax.experimental import pallas as pl
from jax.experimental.pallas import tpu as pltpu

F32 = jnp.float32
BF16 = jnp.bfloat16

D_MODEL = 1024
DEPTH = 4
EPS = 1e-6
MLA_HEADS = 8
MLA_Q_LORA = 256
MLA_KV_LORA = 128
MLA_NOPE = 64
MLA_ROPE = 32
MLA_V = 64
ROPE_BASE = 10000.0
DIL_HEADS = 8
DIL_HEAD_DIM = 64
DIL_PATTERNS = ((128, 1), (512, 4), (2048, 16))
SWA_Q_HEADS = 16
SWA_KV_HEADS = 2
SWA_HEAD_DIM = 64
SWA_WINDOW = 128
D_FF = 2816
MLA_QK = MLA_NOPE + MLA_ROPE
DIL_WIDTH = DIL_HEADS * DIL_HEAD_DIM

LANES = 128
V7X_VMEM_LIMIT = 56 * 1024 * 1024

BAND = 128
BAND_BLOCKS_PER_STEP = 4
ROW_TILE = 256
POST_TILE = 512
MLA_HEADS_PER_STEP = 4
SUM_ROWS = 16
NEG = -1e30
LOG2E = 1.4426950408889634


def _dot(a, b):
    return jnp.dot(a, b, preferred_element_type=F32)


def _dot_nt(a, b):
    return lax.dot_general(a, b, (((1,), (1,)), ((), ())), preferred_element_type=F32)


def _rms(x, g):
    return x * lax.rsqrt(jnp.mean(x * x, axis=-1, keepdims=True) + EPS) * g


def _resident(shape):
    return pl.BlockSpec(shape, lambda *_: (0,) * len(shape), pipeline_mode=pl.Buffered(1))


def _params(sem):
    return pltpu.CompilerParams(dimension_semantics=sem, vmem_limit_bytes=V7X_VMEM_LIMIT)


def _rope_group(x, ct, st, low):
    swapped = jnp.where(low, pltpu.roll(x, 112, 1), pltpu.roll(x, 16, 1))
    return x * ct + swapped * st


def _even_proj_kernel(*refs, q_scale, b_scale, dils):
    (h_ref, gn_ref, wa_ref, wb_ref, gq_ref, gkv_ref, wuq_ref, wuk_ref, wuvt_ref, ct_ref, st_ref,
     q_ref, k_ref, vt_ref) = refs[:14]
    dil_refs = refs[14:14 + 3 * len(dils)]
    pb_ref = refs[-1]
    tm = h_ref.shape[0]
    nb = DIL_WIDTH

    hn = _rms(h_ref[...], gn_ref[...]).astype(BF16)
    pa = _dot(hn, wa_ref[...])
    pb = _dot(hn, wb_ref[...])
    groups = nb // LANES
    for g in range(3 * groups):
        cols = pb[:, LANES * g:LANES * (g + 1)]
        pb_ref[g] = cols * b_scale if g < groups else cols
    for n, d in enumerate(dils):
        for r in range(d):
            rows = pl.ds(r, tm // d, stride=d) if d > 1 else slice(None)
            for part in range(3):
                picked = [pb_ref[part * groups + g, rows, :] for g in range(groups)]
                dil_refs[3 * n + part][r] = jnp.concatenate(picked, axis=1).astype(BF16)

    nq = _rms(pa[:, :MLA_Q_LORA], gq_ref[...]).astype(BF16)
    nkv = _rms(pa[:, MLA_Q_LORA:MLA_Q_LORA + MLA_KV_LORA], gkv_ref[...]).astype(BF16)
    ct = ct_ref[...]
    st = st_ref[...]
    low = lax.broadcasted_iota(jnp.int32, ct.shape, 1) < MLA_NOPE + MLA_ROPE // 2
    q = _dot(nq, wuq_ref[...])
    k = _dot(nkv, wuk_ref[...])
    k_rope = _rope_group(pa[:, MLA_Q_LORA + MLA_KV_LORA:], ct, st, low)
    for h in range(MLA_HEADS):
        sl = slice(LANES * h, LANES * (h + 1))
        q_ref[:, sl] = (_rope_group(q[:, sl], ct, st, low) * q_scale).astype(BF16)
        k_ref[:, sl] = (k[:, sl] + k_rope).astype(BF16)
    vt_ref[0] = _dot_nt(wuvt_ref[...], nkv).astype(BF16)


def _even_proj(h, gn, wa, wb, gq, gkv, wuq, wuk, wuvt, ct, st, dils):
    S = h.shape[0]
    tm = ROW_TILE
    nb = DIL_WIDTH
    row = lambda w: pl.BlockSpec((tm, w), lambda i: (i, 0))
    kern = functools.partial(_even_proj_kernel, q_scale=LOG2E / math.sqrt(MLA_QK),
                             b_scale=1.0 / math.sqrt(DIL_HEAD_DIM), dils=dils)
    dil_specs, dil_shapes = [], []
    for d in dils:
        dil_specs += [pl.BlockSpec((d, tm // d, nb), lambda i: (0, i, 0))] * 3
        dil_shapes += [jax.ShapeDtypeStruct((d, S // d, nb), BF16)] * 3
    out = pl.pallas_call(
        kern,
        grid=(S // tm,),
        in_specs=[row(D_MODEL), _resident(gn.shape), _resident(wa.shape), _resident(wb.shape),
                  _resident(gq.shape), _resident(gkv.shape), _resident(wuq.shape), _resident(wuk.shape),
                  _resident(wuvt.shape), row(LANES), row(LANES)],
        out_specs=[row(MLA_HEADS * LANES), row(MLA_HEADS * LANES),
                   pl.BlockSpec((1, MLA_HEADS * MLA_V, tm), lambda i: (i, 0, 0))] + dil_specs,
        out_shape=[jax.ShapeDtypeStruct((S, MLA_HEADS * LANES), BF16),
                   jax.ShapeDtypeStruct((S, MLA_HEADS * LANES), BF16),
                   jax.ShapeDtypeStruct((S // tm, MLA_HEADS * MLA_V, tm), BF16)] + dil_shapes,
        scratch_shapes=[pltpu.VMEM((3 * nb // LANES, tm, LANES), F32)],
        compiler_params=_params(("parallel",)),
        name="even_proj",
    )(h, gn, wa, wb, gq, gkv, wuq, wuk, wuvt, ct, st)
    return out[:3], [out[3 + 3 * n:6 + 3 * n] for n in range(len(dils))]


def _mla_kernel(q_ref, k_ref, vt_ref, o_ref, s_ref, p_ref, acc_ref, *, n_heads):
    tq = q_ref.shape[0]
    tk = vt_ref.shape[2]
    assert tq == 2 * tk
    i = pl.program_id(1)
    heads = range(n_heads)
    lanes = lambda hh: slice(LANES * hh, LANES * (hh + 1))
    rows = lambda hh: slice(MLA_V * hh, MLA_V * (hh + 1))

    def scores(c, slot):
        start = pl.multiple_of(c * tk, tk)
        for hh in heads:
            s_ref[slot, hh] = _dot_nt(k_ref[pl.ds(start, tk), lanes(hh)], q_ref[:, lanes(hh)])

    def softmax(slot, ms, mask):
        alphas, ms_new = [], []
        for hh in heads:
            s = s_ref[slot, hh]
            if mask is not None:
                s = jnp.where(mask, s, NEG)
            m_new = jnp.maximum(ms[hh], jnp.max(s, axis=0, keepdims=True))
            p_ref[slot, hh] = jnp.exp2(s - m_new).astype(BF16)
            alphas.append(jnp.exp2(ms[hh] - m_new))
            ms_new.append(m_new)
        return alphas, ms_new

    ones = jnp.ones((SUM_ROWS, tk), BF16)

    def values(c, slot, alphas):
        for hh in heads:
            v_ones = jnp.concatenate([vt_ref[c, rows(hh), :], ones], axis=0)
            acc_ref[hh] = alphas[hh] * acc_ref[hh] + _dot(v_ones, p_ref[slot, hh])

    def pair(t, carry):
        ms, al_odd = carry
        c = 2 * t
        al_even, ms = softmax(0, ms, None)
        scores(c + 1, 1)
        values(jnp.maximum(c - 1, 0), 1, al_odd)
        al_odd, ms = softmax(1, ms, None)
        scores(c + 2, 0)
        values(c, 0, al_even)
        return ms, al_odd

    acc_ref[...] = jnp.zeros(acc_ref.shape, F32)
    p_ref[1] = jnp.zeros(p_ref.shape[1:], BF16)
    scores(0, 0)
    init = ([jnp.full((1, tq), NEG, F32) for _ in heads], [jnp.ones((1, tq), F32) for _ in heads])
    ms, al_odd = lax.fori_loop(0, i, pair, init)

    c = 2 * i
    row = lax.broadcasted_iota(jnp.int32, (tk, tq), 0)
    col = lax.broadcasted_iota(jnp.int32, (tk, tq), 1)
    values(jnp.maximum(c - 1, 0), 1, al_odd)
    scores(c + 1, 1)
    al_even, ms = softmax(0, ms, row <= col)
    values(c, 0, al_even)
    al_odd, ms = softmax(1, ms, row + tk <= col)
    values(c + 1, 1, al_odd)
    out_t = jnp.concatenate([acc_ref[hh, :MLA_V] / acc_ref[hh, MLA_V:MLA_V + 1] for hh in heads], axis=0)
    o_ref[...] = out_t.T.astype(BF16)


def _mla_attention(q, k, vt):
    S = q.shape[0]
    nh = MLA_HEADS_PER_STEP
    n_chunks, _, tk = vt.shape
    tq = 2 * tk
    assert n_chunks * tk == S and S % tq == 0
    return pl.pallas_call(
        functools.partial(_mla_kernel, n_heads=nh),
        grid=(MLA_HEADS // nh, S // tq),
        in_specs=[pl.BlockSpec((tq, nh * LANES), lambda hp, i: (i, hp)),
                  pl.BlockSpec((S, nh * LANES), lambda hp, i: (0, hp), pipeline_mode=pl.Buffered(1)),
                  pl.BlockSpec((n_chunks, nh * MLA_V, tk), lambda hp, i: (0, hp, 0),
                               pipeline_mode=pl.Buffered(1))],
        out_specs=pl.BlockSpec((tq, nh * MLA_V), lambda hp, i: (i, hp)),
        out_shape=jax.ShapeDtypeStruct((S, MLA_HEADS * MLA_V), BF16),
        scratch_shapes=[pltpu.VMEM((2, nh, tk, tq), F32), pltpu.VMEM((2, nh, tk, tq), BF16),
                        pltpu.VMEM((nh, MLA_V + SUM_ROWS, tq), F32)],
        compiler_params=_params(("parallel", "arbitrary")),
        name="mla_attention",
    )(q, k, vt)


def _band_kernel(*refs, n_pairs, q_pairs_per_kv, slopes, extra_key, use_sink, with_lse, n_sub):
    refs = list(refs)
    sinks_ref = refs.pop(0) if use_sink else None
    q_ref, kp_ref, kc_ref, vp_ref, vc_ref = refs[:5]
    refs = refs[5:]
    o_ref = refs.pop(0)
    lse_ref = refs.pop(0) if with_lse else None
    kall_ref, vall_ref = refs

    step = pl.program_id(1)
    tq = BAND
    half = LANES // 2
    qi = lax.broadcasted_iota(jnp.int32, (tq, tq), 0)
    kj = lax.broadcasted_iota(jnp.int32, (tq, tq), 1)
    tri = kj <= qi
    diag = kj == qi
    distf = jnp.where(tri, qi - kj, qi - kj + tq).astype(F32)
    upper_only = jnp.where(tri, 0.0, NEG)
    kv_lane = lax.broadcasted_iota(jnp.int32, (2 * tq, LANES), 1)
    kv_low = kv_lane < half
    q_low = lax.broadcasted_iota(jnp.int32, (tq, LANES), 1) < half
    head_lane = lax.broadcasted_iota(jnp.int32, (tq, LANES), 1)
    zero_kv = jnp.zeros((2 * tq, LANES), BF16)
    rider = (jnp.where(kv_lane == half, 1.0, 0.0).astype(BF16), jnp.where(kv_lane == 0, 1.0, 0.0).astype(BF16))
    rider_lane = (half, 0)
    n_heads = 2 * n_pairs

    kall_ref[:tq] = kp_ref[...]
    kall_ref[tq:] = kc_ref[...]
    vall_ref[:tq] = vp_ref[...]
    vall_ref[tq:] = vc_ref[...]

    def block(j, carry):
        i = step * n_sub + j
        row0 = pl.multiple_of(j * tq, tq)
        first_mask = jnp.where(i > 0, 0.0, upper_only)
        kv_cache = {}

        def kv_for(kvp):
            if kvp not in kv_cache:
                sl = slice(LANES * kvp, LANES * (kvp + 1))
                kb = kall_ref[pl.ds(row0, 2 * tq), sl]
                vb = vall_ref[pl.ds(row0, 2 * tq), sl]
                kv_cache[kvp] = ((jnp.where(kv_low, kb, zero_kv), jnp.where(kv_low, zero_kv, kb)),
                                 (jnp.where(kv_low, vb, rider[0]), jnp.where(kv_low, rider[1], vb)))
            return kv_cache[kvp]

        scores = []
        for h in range(n_heads):
            kz, _ = kv_for((h // 2) // q_pairs_per_kv)
            qp = q_ref[pl.ds(row0, tq), LANES * (h // 2):LANES * (h // 2 + 1)]
            scores.append(_dot_nt(qp, kz[h % 2]))

        probs, maxes, fars = [], [], []
        for h in range(n_heads):
            s2 = scores[h]
            s = jnp.where(tri, s2[:, tq:], s2[:, :tq]) + (distf * (-slopes[h]) + first_mask)
            m = jnp.max(s, axis=1, keepdims=True)
            if extra_key:
                far_h = jnp.sum(jnp.where(diag, s2[:, :tq], 0.0), axis=1, keepdims=True) - slopes[h] * tq
                far_h = jnp.where(i > 0, far_h, NEG)
                m = jnp.maximum(m, far_h)
            if use_sink:
                m = jnp.maximum(m, sinks_ref[h])
            p = jnp.exp(s - m)
            zero = jnp.zeros_like(p)
            probs.append(jnp.concatenate([jnp.where(tri, zero, p), jnp.where(tri, p, zero)],
                                         axis=1).astype(BF16))
            maxes.append(m)
            fars.append(jnp.exp(far_h - m) if extra_key else None)

        lse = jnp.zeros((tq, LANES), F32)
        for pr in range(n_pairs):
            sl = slice(LANES * pr, LANES * (pr + 1))
            _, vz = kv_for(pr // q_pairs_per_kv)
            pv, ls = [], []
            for e in range(2):
                h = 2 * pr + e
                out = _dot(probs[h], vz[e])
                l = out[:, rider_lane[e]:rider_lane[e] + 1]
                if extra_key:
                    l = l + fars[h]
                if use_sink:
                    l = l + jnp.exp(sinks_ref[h] - maxes[h])
                pv.append(out)
                ls.append(l)
                if with_lse:
                    lse = jnp.where(head_lane == h, maxes[h] + jnp.log(l), lse)
            acc = jnp.where(q_low, pv[0], pv[1])
            if extra_key:
                v_far = vall_ref[pl.ds(row0, tq), sl].astype(F32)
                acc = acc + v_far * jnp.where(q_low, fars[2 * pr], fars[2 * pr + 1])
            o_ref[pl.ds(row0, tq), sl] = (acc * jnp.where(q_low, 1.0 / ls[0], 1.0 / ls[1])).astype(BF16)
        if with_lse:
            lse_ref[pl.ds(row0, tq), :] = lse
        return carry

    lax.fori_loop(0, n_sub, block, 0)


def _band_attention(q, k, v, *, slopes, max_dist, sinks=None, with_lse=False):
    d, T, wq = q.shape
    wk = k.shape[2]
    n_sub = BAND_BLOCKS_PER_STEP
    rows = BAND * n_sub
    assert max_dist in (BAND - 1, BAND) and T % rows == 0
    extra_key = max_dist == BAND
    n_pairs = wq // LANES
    q_pairs_per_kv = n_pairs // (wk // LANES)
    cur = lambda w: pl.BlockSpec((None, rows, w), lambda r, i: (r, i, 0))
    prev = lambda w: pl.BlockSpec((None, BAND, w), lambda r, i: (r, jnp.maximum(i * n_sub - 1, 0), 0))

    args, in_specs = [], []
    if sinks is not None:
        args.append(sinks)
        in_specs.append(pl.BlockSpec(memory_space=pltpu.SMEM))
    args += [q, k, k, v, v]
    in_specs += [cur(wq), prev(wk), cur(wk), prev(wk), cur(wk)]
    if extra_key:
        assert wq == wk
    out_shape = [jax.ShapeDtypeStruct((d, T, wq), BF16)]
    out_specs = [cur(wq)]
    if with_lse:
        out_shape.append(jax.ShapeDtypeStruct((d, T, LANES), F32))
        out_specs.append(cur(LANES))
    kern = functools.partial(_band_kernel, n_pairs=n_pairs, q_pairs_per_kv=q_pairs_per_kv,
                             slopes=tuple(float(s) for s in slopes), extra_key=extra_key,
                             use_sink=sinks is not None, with_lse=with_lse, n_sub=n_sub)
    out = pl.pallas_call(
        kern,
        grid=(d, T // rows),
        in_specs=in_specs,
        out_specs=out_specs,
        out_shape=out_shape,
        scratch_shapes=[pltpu.VMEM((BAND + rows, wk), BF16), pltpu.VMEM((BAND + rows, wk), BF16)],
        compiler_params=_params(("parallel", "parallel")),
        name=f"band_d{d}" + ("_sink" if sinks is not None else ""),
    )(*args)
    return out if with_lse else out[0]


def _odd_proj_kernel(h_ref, gn_ref, wq_ref, wk_ref, wv_ref, q_ref, k_ref, v_ref, *, q_scale):
    hn = _rms(h_ref[...], gn_ref[...]).astype(BF16)
    q_ref[...] = (_dot(hn, wq_ref[...]) * q_scale).astype(BF16)
    k_ref[...] = _dot(hn, wk_ref[...]).astype(BF16)
    v_ref[...] = _dot(hn, wv_ref[...]).astype(BF16)


def _odd_proj(h, gn, wq, wk, wv):
    S = h.shape[0]
    tm = ROW_TILE
    row = lambda w: pl.BlockSpec((tm, w), lambda i: (i, 0))
    kern = functools.partial(_odd_proj_kernel, q_scale=1.0 / math.sqrt(SWA_HEAD_DIM))
    return pl.pallas_call(
        kern,
        grid=(S // tm,),
        in_specs=[row(D_MODEL), _resident(gn.shape), _resident(wq.shape), _resident(wk.shape),
                  _resident(wv.shape)],
        out_specs=[row(wq.shape[1]), row(wk.shape[1]), row(wv.shape[1])],
        out_shape=[jax.ShapeDtypeStruct((S, wq.shape[1]), BF16),
                   jax.ShapeDtypeStruct((S, wk.shape[1]), BF16),
                   jax.ShapeDtypeStruct((S, wv.shape[1]), BF16)],
        compiler_params=_params(("parallel",)),
        name="odd_proj",
    )(h, gn, wq, wk, wv)


def _merge_patterns(pat_refs, expand_ref, nat_refs, dils, tm):
    outs, lses = [], []
    nat = iter(nat_refs)
    for n, d in enumerate(dils):
        o_ref, l_ref = pat_refs[2 * n], pat_refs[2 * n + 1]
        if d == 1:
            outs.append(o_ref[0].astype(F32))
            lses.append(l_ref[0])
        else:
            o_nat, l_nat = next(nat), next(nat)
            for r in range(d):
                rows = pl.ds(r, tm // d, stride=d)
                block = o_ref[r].astype(F32)
                for g in range(o_nat.shape[0]):
                    o_nat[g, rows, :] = block[:, LANES * g:LANES * (g + 1)]
                l_nat[rows, :] = l_ref[r]
            outs.append(jnp.concatenate([o_nat[g] for g in range(o_nat.shape[0])], axis=1))
            lses.append(l_nat[...])
    top = functools.reduce(jnp.maximum, lses)
    es = [jnp.exp(l - top) for l in lses]
    inv = 1.0 / functools.reduce(jnp.add, es)
    merged = None
    for o, e in zip(outs, es):
        w = e * inv
        w_hi = w.astype(BF16)
        w_lo = (w - w_hi.astype(F32)).astype(BF16)
        w_full = _dot(w_hi, expand_ref[...]) + _dot(w_lo, expand_ref[...])
        merged = o * w_full if merged is None else merged + o * w_full
    return merged.astype(BF16)


def _post_kernel(*refs, final_norm, dils):
    refs = list(refs)
    h_ref, a1_ref = refs[:2]
    refs = refs[2:]
    tm = h_ref.shape[0]
    if dils:
        pat_refs, expand_ref = refs[:2 * len(dils)], refs[2 * len(dils)]
        refs = refs[2 * len(dils) + 1:]
    else:
        a2_ref = refs.pop(0)
    wo1_ref, wo2_ref, gf_ref, wgu_ref, wd_ref = refs[:5]
    refs = refs[5:]
    gfin_ref = refs.pop(0) if final_norm else None
    o_ref = refs.pop(0)
    a2 = _merge_patterns(pat_refs, expand_ref, refs, dils, tm) if dils else a2_ref[...]

    h1 = h_ref[...] + _dot(a1_ref[...], wo1_ref[...]) + _dot(a2, wo2_ref[...])
    hn = _rms(h1, gf_ref[...]).astype(BF16)
    gate = _dot(hn, wgu_ref[:, :D_FF])
    up = _dot(hn, wgu_ref[:, D_FF:])
    act = (gate / (1.0 + jnp.exp(-gate)) * up).astype(BF16)
    out = h1 + _dot(act, wd_ref[...])
    if final_norm:
        out = _rms(out, gfin_ref[...])
    o_ref[...] = out


def _post(h, a1, a2, wo1, wo2, gf, wgu, wd, gfin=None):
    S = h.shape[0]
    tm = POST_TILE
    w_half = wo1.shape[0]
    row = lambda w, c=0: pl.BlockSpec((tm, w), lambda i, c=c: (i, c))
    args, in_specs, scratch, dils = [h, a1], [row(D_MODEL), row(w_half, 0)], [], ()
    if isinstance(a2, list):
        dils = tuple(d for d, _, _ in a2)
        for d, o, lse in a2:
            args += [o, lse]
            in_specs += [pl.BlockSpec((d, tm // d, w_half), lambda i: (0, i, 0)),
                         pl.BlockSpec((d, tm // d, LANES), lambda i: (0, i, 0))]
            if d > 1:
                scratch += [pltpu.VMEM((w_half // LANES, tm, LANES), F32), pltpu.VMEM((tm, LANES), F32)]
        head_of_lane = jnp.arange(w_half) // DIL_HEAD_DIM
        expand = (jnp.arange(LANES)[:, None] == head_of_lane[None, :]).astype(BF16)
        args.append(expand)
        in_specs.append(_resident(expand.shape))
    else:
        args.append(a2)
        in_specs.append(row(w_half, 1 if a2 is a1 else 0))
    args += [wo1, wo2, gf, wgu, wd]
    in_specs += [_resident(wo1.shape), _resident(wo2.shape), _resident(gf.shape),
                 _resident(wgu.shape), _resident(wd.shape)]
    if gfin is not None:
        args.append(gfin)
        in_specs.append(_resident(gfin.shape))
    return pl.pallas_call(
        functools.partial(_post_kernel, final_norm=gfin is not None, dils=dils),
        grid=(S // tm,),
        in_specs=in_specs,
        out_specs=row(D_MODEL),
        out_shape=jax.ShapeDtypeStruct((S, D_MODEL), F32),
        scratch_shapes=scratch,
        compiler_params=_params(("parallel",)),
        name=("post_merge" if dils else "post") + ("_final" if gfin is not None else ""),
    )(*args)


def _alibi_slopes(n):
    return [2.0 ** (-8.0 * (i + 1) / n) for i in range(n)]


def _rope_tables(S):
    pos = jnp.arange(S, dtype=F32)
    inv_freq = ROPE_BASE ** (-jnp.arange(0, MLA_ROPE, 2, dtype=F32) / MLA_ROPE)
    ang = pos[:, None] * inv_freq[None, :]
    cos, sin = jnp.cos(ang), jnp.sin(ang)
    ones = jnp.ones((S, MLA_NOPE), F32)
    zeros = jnp.zeros((S, MLA_NOPE), F32)
    pad = jnp.zeros((S, LANES - MLA_QK), F32)
    return (jnp.concatenate([ones, cos, cos, pad], axis=1),
            jnp.concatenate([zeros, -sin, sin, pad], axis=1))


def _pad_heads(w, heads, width):
    rows = w.shape[0]
    w = w.reshape(rows, heads, width)
    return jnp.pad(w, ((0, 0), (0, 0), (0, LANES - width))).reshape(rows, heads * LANES)


def _even_weights(w_in, w_uq, w_ukv):
    o2 = MLA_Q_LORA + MLA_KV_LORA
    o3 = o2 + MLA_ROPE
    z = lambda n: jnp.zeros((D_MODEL, n), F32)
    wa = jnp.concatenate([w_in[:, :o2], z(MLA_NOPE), w_in[:, o2:o3], z(LANES - MLA_QK)], axis=1)
    wb = w_in[:, o3:]
    wuq = _pad_heads(w_uq, MLA_HEADS, MLA_QK)
    wukv = w_ukv.reshape(MLA_KV_LORA, MLA_HEADS, MLA_NOPE + MLA_V)
    wuk = _pad_heads(wukv[..., :MLA_NOPE].reshape(MLA_KV_LORA, -1), MLA_HEADS, MLA_NOPE)
    wuvt = wukv[..., MLA_NOPE:].reshape(MLA_KV_LORA, -1).T
    return [w.astype(BF16) for w in (wa, wb, wuq, wuk, wuvt)]


def _odd_weights(w_qkv):
    nq = SWA_Q_HEADS * SWA_HEAD_DIM
    nk = SWA_KV_HEADS * SWA_HEAD_DIM
    dup = lambda w: jnp.repeat(w.reshape(D_MODEL, SWA_KV_HEADS, 1, SWA_HEAD_DIM), 2, axis=2).reshape(D_MODEL, -1)
    return [w.astype(BF16) for w in (w_qkv[:, :nq], dup(w_qkv[:, nq:nq + nk]), dup(w_qkv[:, nq + nk:]))]


def kernel(x, attn_norm, ffn_norm, final_norm, e_w_in, e_q_norm, e_kv_norm, e_w_uq, e_w_ukv, e_w_out,
           o_w_qkv, o_sinks, o_w_out, f_w_gate_up, f_w_down):
    B, S, D = x.shape
    dils = tuple(d for _, d in DIL_PATTERNS)
    assert B == 1 and D == D_MODEL and S % POST_TILE == 0 and S % (BAND * max(dils)) == 0
    h = x.reshape(S, D)
    ct, st = _rope_tables(S)
    dil_slopes = _alibi_slopes(DIL_HEADS)
    swa_slopes = _alibi_slopes(SWA_Q_HEADS)
    row = lambda g: g.reshape(1, -1)

    for layer in range(DEPTH):
        i = layer // 2
        if layer % 2 == 0:
            wa, wb, wuq, wuk, wuvt = _even_weights(e_w_in[i], e_w_uq[i], e_w_ukv[i])
            (q, k, vt), dilated = _even_proj(h, row(attn_norm[layer]), wa, wb, row(e_q_norm[i]),
                                             row(e_kv_norm[i]), wuq, wuk, wuvt, ct, st, dils)
            a1 = _mla_attention(q, k, vt)
            a2 = []
            for (window, d), (qd, kd, vd) in zip(DIL_PATTERNS, dilated):
                o, lse = _band_attention(qd, kd, vd, slopes=[s * d for s in dil_slopes],
                                         max_dist=window // d, with_lse=True)
                a2.append((d, o, lse))
            w_out = e_w_out[i].astype(BF16)
            half = MLA_HEADS * MLA_V
        else:
            wq, wk, wv = _odd_weights(o_w_qkv[i])
            q, k, v = _odd_proj(h, row(attn_norm[layer]), wq, wk, wv)
            a1 = a2 = _band_attention(q[None], k[None], v[None], slopes=swa_slopes,
                                      max_dist=SWA_WINDOW - 1, sinks=o_sinks[i])[0]
            w_out = o_w_out[i].astype(BF16)
            half = SWA_Q_HEADS * SWA_HEAD_DIM // 2
        h = _post(h, a1, a2, w_out[:half], w_out[half:], row(ffn_norm[layer]),
                  f_w_gate_up[layer].astype(BF16), f_w_down[layer].astype(BF16),
                  row(final_norm) if layer == DEPTH - 1 else None)
    return h.reshape(B, S, D)
```

```python
import functools
import math

import jax
import jax.numpy as jnp
from jax import lax
from jax.experimental import pallas as pl
from jax.experimental.pallas import tpu as pltpu

F32 = jnp.float32
BF16 = jnp.bfloat16

D_MODEL = 1024
DEPTH = 4
EPS = 1e-6
MLA_HEADS = 8
MLA_Q_LORA = 256
MLA_KV_LORA = 128
MLA_NOPE = 64
MLA_ROPE = 32
MLA_V = 64
ROPE_BASE = 10000.0
DIL_HEADS = 8
DIL_HEAD_DIM = 64
DIL_PATTERNS = ((128, 1), (512, 4), (2048, 16))
SWA_Q_HEADS = 16
SWA_KV_HEADS = 2
SWA_HEAD_DIM = 64
SWA_WINDOW = 128
D_FF = 2816
MLA_QK = MLA_NOPE + MLA_ROPE
DIL_WIDTH = DIL_HEADS * DIL_HEAD_DIM

LANES = 128
V7X_VMEM_LIMIT = 56 * 1024 * 1024

BAND = 128
BAND_BLOCKS_PER_STEP = 4
ROW_TILE = 512
POST_TILE = 512
MLA_HEADS_PER_STEP = 2
SUM_ROWS = 16
NEG = -1e30
LOG2E = 1.4426950408889634


def _dot(a, b):
    return jnp.dot(a, b, preferred_element_type=F32)


def _dot_nt(a, b):
    return lax.dot_general(a, b, (((1,), (1,)), ((), ())), preferred_element_type=F32)


def _rms(x, g):
    return x * lax.rsqrt(jnp.mean(x * x, axis=-1, keepdims=True) + EPS) * g


def _resident(shape):
    return pl.BlockSpec(shape, lambda *_: (0,) * len(shape), pipeline_mode=pl.Buffered(1))


def _params(sem):
    return pltpu.CompilerParams(dimension_semantics=sem, vmem_limit_bytes=V7X_VMEM_LIMIT)


def _rope_group(x, ct, st, low):
    swapped = jnp.where(low, pltpu.roll(x, 112, 1), pltpu.roll(x, 16, 1))
    return x * ct + swapped * st


def _even_proj_kernel(*refs, q_scale, b_scale, dils):
    (h_ref, gn_ref, wa_ref, wb_ref, gq_ref, gkv_ref, wuq_ref, wuk_ref, wuvt_ref, ct_ref, st_ref,
     q_ref, k_ref, vt_ref) = refs[:14]
    dil_refs = refs[14:14 + 3 * len(dils)]
    pb_ref = refs[-1]
    tm = h_ref.shape[0]
    nb = DIL_WIDTH

    hn = _rms(h_ref[...], gn_ref[...]).astype(BF16)
    pa = _dot(hn, wa_ref[...])
    pb = _dot(hn, wb_ref[...])
    groups = nb // LANES
    for g in range(3 * groups):
        cols = pb[:, LANES * g:LANES * (g + 1)]
        pb_ref[g] = cols * b_scale if g < groups else cols
    for n, d in enumerate(dils):
        for r in range(d):
            rows = pl.ds(r, tm // d, stride=d) if d > 1 else slice(None)
            for part in range(3):
                picked = [pb_ref[part * groups + g, rows, :] for g in range(groups)]
                dil_refs[3 * n + part][r] = jnp.concatenate(picked, axis=1).astype(BF16)

    nq = _rms(pa[:, :MLA_Q_LORA], gq_ref[...]).astype(BF16)
    nkv = _rms(pa[:, MLA_Q_LORA:MLA_Q_LORA + MLA_KV_LORA], gkv_ref[...]).astype(BF16)
    ct = ct_ref[...]
    st = st_ref[...]
    low = lax.broadcasted_iota(jnp.int32, ct.shape, 1) < MLA_NOPE + MLA_ROPE // 2
    q = _dot(nq, wuq_ref[...])
    k = _dot(nkv, wuk_ref[...])
    k_rope = _rope_group(pa[:, MLA_Q_LORA + MLA_KV_LORA:], ct, st, low)
    for h in range(MLA_HEADS):
        sl = slice(LANES * h, LANES * (h + 1))
        q_ref[:, sl] = (_rope_group(q[:, sl], ct, st, low) * q_scale).astype(BF16)
        k_ref[:, sl] = (k[:, sl] + k_rope).astype(BF16)
    vt_ref[0] = _dot_nt(wuvt_ref[...], nkv).astype(BF16)


def _even_proj(h, gn, wa, wb, gq, gkv, wuq, wuk, wuvt, ct, st, dils):
    S = h.shape[0]
    tm = ROW_TILE
    nb = DIL_WIDTH
    row = lambda w: pl.BlockSpec((tm, w), lambda i: (i, 0))
    kern = functools.partial(_even_proj_kernel, q_scale=LOG2E / math.sqrt(MLA_QK),
                             b_scale=1.0 / math.sqrt(DIL_HEAD_DIM), dils=dils)
    dil_specs, dil_shapes = [], []
    for d in dils:
        dil_specs += [pl.BlockSpec((d, tm // d, nb), lambda i: (0, i, 0))] * 3
        dil_shapes += [jax.ShapeDtypeStruct((d, S // d, nb), BF16)] * 3
    out = pl.pallas_call(
        kern,
        grid=(S // tm,),
        in_specs=[row(D_MODEL), _resident(gn.shape), _resident(wa.shape), _resident(wb.shape),
                  _resident(gq.shape), _resident(gkv.shape), _resident(wuq.shape), _resident(wuk.shape),
                  _resident(wuvt.shape), row(LANES), row(LANES)],
        out_specs=[row(MLA_HEADS * LANES), row(MLA_HEADS * LANES),
                   pl.BlockSpec((1, MLA_HEADS * MLA_V, tm), lambda i: (i, 0, 0))] + dil_specs,
        out_shape=[jax.ShapeDtypeStruct((S, MLA_HEADS * LANES), BF16),
                   jax.ShapeDtypeStruct((S, MLA_HEADS * LANES), BF16),
                   jax.ShapeDtypeStruct((S // tm, MLA_HEADS * MLA_V, tm), BF16)] + dil_shapes,
        scratch_shapes=[pltpu.VMEM((3 * nb // LANES, tm, LANES), F32)],
        compiler_params=_params(("parallel",)),
        name="even_proj",
    )(h, gn, wa, wb, gq, gkv, wuq, wuk, wuvt, ct, st)
    return out[:3], [out[3 + 3 * n:6 + 3 * n] for n in range(len(dils))]


def _mla_kernel(q_ref, k_ref, vt_ref, o_ref, s_ref, p_ref, acc_ref, *, n_heads):
    tq = q_ref.shape[0]
    tk = vt_ref.shape[2]
    assert tq == 2 * tk
    i = pl.program_id(1)
    heads = range(n_heads)
    lanes = lambda hh: slice(LANES * hh, LANES * (hh + 1))
    rows = lambda hh: slice(MLA_V * hh, MLA_V * (hh + 1))

    def scores(c, slot):
        start = pl.multiple_of(c * tk, tk)
        for hh in heads:
            s_ref[slot, hh] = _dot_nt(k_ref[pl.ds(start, tk), lanes(hh)], q_ref[:, lanes(hh)])

    def softmax(slot, ms, mask):
        alphas, ms_new = [], []
        for hh in heads:
            s = s_ref[slot, hh]
            if mask is not None:
                s = jnp.where(mask, s, NEG)
            m_new = jnp.maximum(ms[hh], jnp.max(s, axis=0, keepdims=True))
            p_ref[slot, hh] = jnp.exp2(s - m_new).astype(BF16)
            alphas.append(jnp.exp2(ms[hh] - m_new))
            ms_new.append(m_new)
        return alphas, ms_new

    ones = jnp.ones((SUM_ROWS, tk), BF16)

    def values(c, slot, alphas):
        for hh in heads:
            v_ones = jnp.concatenate([vt_ref[c, rows(hh), :], ones], axis=0)
            acc_ref[hh] = alphas[hh] * acc_ref[hh] + _dot(v_ones, p_ref[slot, hh])

    def pair(t, carry):
        ms, al_odd = carry
        c = 2 * t
        al_even, ms = softmax(0, ms, None)
        scores(c + 1, 1)
        values(jnp.maximum(c - 1, 0), 1, al_odd)
        al_odd, ms = softmax(1, ms, None)
        scores(c + 2, 0)
        values(c, 0, al_even)
        return ms, al_odd

    acc_ref[...] = jnp.zeros(acc_ref.shape, F32)
    p_ref[1] = jnp.zeros(p_ref.shape[1:], BF16)
    scores(0, 0)
    init = ([jnp.full((1, tq), NEG, F32) for _ in heads], [jnp.ones((1, tq), F32) for _ in heads])
    ms, al_odd = lax.fori_loop(0, i, pair, init)

    c = 2 * i
    row = lax.broadcasted_iota(jnp.int32, (tk, tq), 0)
    col = lax.broadcasted_iota(jnp.int32, (tk, tq), 1)
    values(jnp.maximum(c - 1, 0), 1, al_odd)
    scores(c + 1, 1)
    al_even, ms = softmax(0, ms, row <= col)
    values(c, 0, al_even)
    al_odd, ms = softmax(1, ms, row + tk <= col)
    values(c + 1, 1, al_odd)
    out_t = jnp.concatenate([acc_ref[hh, :MLA_V] / acc_ref[hh, MLA_V:MLA_V + 1] for hh in heads], axis=0)
    o_ref[...] = out_t.T.astype(BF16)


def _mla_attention(q, k, vt):
    S = q.shape[0]
    nh = MLA_HEADS_PER_STEP
    n_chunks, _, tk = vt.shape
    tq = 2 * tk
    assert n_chunks * tk == S and S % tq == 0
    return pl.pallas_call(
        functools.partial(_mla_kernel, n_heads=nh),
        grid=(MLA_HEADS // nh, S // tq),
        in_specs=[pl.BlockSpec((tq, nh * LANES), lambda hp, i: (i, hp)),
                  pl.BlockSpec((S, nh * LANES), lambda hp, i: (0, hp), pipeline_mode=pl.Buffered(1)),
                  pl.BlockSpec((n_chunks, nh * MLA_V, tk), lambda hp, i: (0, hp, 0),
                               pipeline_mode=pl.Buffered(1))],
        out_specs=pl.BlockSpec((tq, nh * MLA_V), lambda hp, i: (i, hp)),
        out_shape=jax.ShapeDtypeStruct((S, MLA_HEADS * MLA_V), BF16),
        scratch_shapes=[pltpu.VMEM((2, nh, tk, tq), F32), pltpu.VMEM((2, nh, tk, tq), BF16),
                        pltpu.VMEM((nh, MLA_V + SUM_ROWS, tq), F32)],
        compiler_params=_params(("parallel", "arbitrary")),
        name="mla_attention",
    )(q, k, vt)


def _band_kernel(*refs, n_pairs, q_pairs_per_kv, slopes, extra_key, use_sink, with_lse, n_sub):
    refs = list(refs)
    sinks_ref = refs.pop(0) if use_sink else None
    q_ref, kp_ref, kc_ref, vp_ref, vc_ref = refs[:5]
    refs = refs[5:]
    o_ref = refs.pop(0)
    lse_ref = refs.pop(0) if with_lse else None
    kall_ref, vall_ref = refs

    step = pl.program_id(1)
    tq = BAND
    half = LANES // 2
    qi = lax.broadcasted_iota(jnp.int32, (tq, tq), 0)
    kj = lax.broadcasted_iota(jnp.int32, (tq, tq), 1)
    tri = kj <= qi
    diag = kj == qi
    distf = jnp.where(tri, qi - kj, qi - kj + tq).astype(F32)
    upper_only = jnp.where(tri, 0.0, NEG)
    kv_lane = lax.broadcasted_iota(jnp.int32, (2 * tq, LANES), 1)
    kv_low = kv_lane < half
    q_low = lax.broadcasted_iota(jnp.int32, (tq, LANES), 1) < half
    head_lane = lax.broadcasted_iota(jnp.int32, (tq, LANES), 1)
    zero_kv = jnp.zeros((2 * tq, LANES), BF16)
    rider = (jnp.where(kv_lane == half, 1.0, 0.0).astype(BF16), jnp.where(kv_lane == 0, 1.0, 0.0).astype(BF16))
    rider_lane = (half, 0)
    n_heads = 2 * n_pairs

    kall_ref[:tq] = kp_ref[...]
    kall_ref[tq:] = kc_ref[...]
    vall_ref[:tq] = vp_ref[...]
    vall_ref[tq:] = vc_ref[...]

    def block(j, carry):
        i = step * n_sub + j
        row0 = pl.multiple_of(j * tq, tq)
        first_mask = jnp.where(i > 0, 0.0, upper_only)
        kv_cache = {}

        def kv_for(kvp):
            if kvp not in kv_cache:
                sl = slice(LANES * kvp, LANES * (kvp + 1))
                kb = kall_ref[pl.ds(row0, 2 * tq), sl]
                vb = vall_ref[pl.ds(row0, 2 * tq), sl]
                kv_cache[kvp] = ((jnp.where(kv_low, kb, zero_kv), jnp.where(kv_low, zero_kv, kb)),
                                 (jnp.where(kv_low, vb, rider[0]), jnp.where(kv_low, rider[1], vb)))
            return kv_cache[kvp]

        scores = []
        for h in range(n_heads):
            kz, _ = kv_for((h // 2) // q_pairs_per_kv)
            qp = q_ref[pl.ds(row0, tq), LANES * (h // 2):LANES * (h // 2 + 1)]
            scores.append(_dot_nt(qp, kz[h % 2]))

        probs, maxes, fars = [], [], []
        for h in range(n_heads):
            s2 = scores[h]
            s = jnp.where(tri, s2[:, tq:], s2[:, :tq]) + (distf * (-slopes[h]) + first_mask)
            m = jnp.max(s, axis=1, keepdims=True)
            if extra_key:
                far_h = jnp.sum(jnp.where(diag, s2[:, :tq], 0.0), axis=1, keepdims=True) - slopes[h] * tq
                far_h = jnp.where(i > 0, far_h, NEG)
                m = jnp.maximum(m, far_h)
            if use_sink:
                m = jnp.maximum(m, sinks_ref[h])
            p = jnp.exp(s - m)
            zero = jnp.zeros_like(p)
            probs.append(jnp.concatenate([jnp.where(tri, zero, p), jnp.where(tri, p, zero)],
                                         axis=1).astype(BF16))
            maxes.append(m)
            fars.append(jnp.exp(far_h - m) if extra_key else None)

        lse = jnp.zeros((tq, LANES), F32)
        for pr in range(n_pairs):
            sl = slice(LANES * pr, LANES * (pr + 1))
            _, vz = kv_for(pr // q_pairs_per_kv)
            pv, ls = [], []
            for e in range(2):
                h = 2 * pr + e
                out = _dot(probs[h], vz[e])
                l = out[:, rider_lane[e]:rider_lane[e] + 1]
                if extra_key:
                    l = l + fars[h]
                if use_sink:
                    l = l + jnp.exp(sinks_ref[h] - maxes[h])
                pv.append(out)
                ls.append(l)
                if with_lse:
                    lse = jnp.where(head_lane == h, maxes[h] + jnp.log(l), lse)
            acc = jnp.where(q_low, pv[0], pv[1])
            if extra_key:
                v_far = vall_ref[pl.ds(row0, tq), sl].astype(F32)
                acc = acc + v_far * jnp.where(q_low, fars[2 * pr], fars[2 * pr + 1])
            o_ref[pl.ds(row0, tq), sl] = (acc * jnp.where(q_low, 1.0 / ls[0], 1.0 / ls[1])).astype(BF16)
        if with_lse:
            lse_ref[pl.ds(row0, tq), :] = lse
        return carry

    lax.fori_loop(0, n_sub, block, 0)


def _band_attention(q, k, v, *, slopes, max_dist, sinks=None, with_lse=False):
    d, T, wq = q.shape
    wk = k.shape[2]
    n_sub = BAND_BLOCKS_PER_STEP
    rows = BAND * n_sub
    assert max_dist in (BAND - 1, BAND) and T % rows == 0
    extra_key = max_dist == BAND
    n_pairs = wq // LANES
    q_pairs_per_kv = n_pairs // (wk // LANES)
    cur = lambda w: pl.BlockSpec((None, rows, w), lambda r, i: (r, i, 0))
    prev = lambda w: pl.BlockSpec((None, BAND, w), lambda r, i: (r, jnp.maximum(i * n_sub - 1, 0), 0))

    args, in_specs = [], []
    if sinks is not None:
        args.append(sinks)
        in_specs.append(pl.BlockSpec(memory_space=pltpu.SMEM))
    args += [q, k, k, v, v]
    in_specs += [cur(wq), prev(wk), cur(wk), prev(wk), cur(wk)]
    if extra_key:
        assert wq == wk
    out_shape = [jax.ShapeDtypeStruct((d, T, wq), BF16)]
    out_specs = [cur(wq)]
    if with_lse:
        out_shape.append(jax.ShapeDtypeStruct((d, T, LANES), F32))
        out_specs.append(cur(LANES))
    kern = functools.partial(_band_kernel, n_pairs=n_pairs, q_pairs_per_kv=q_pairs_per_kv,
                             slopes=tuple(float(s) for s in slopes), extra_key=extra_key,
                             use_sink=sinks is not None, with_lse=with_lse, n_sub=n_sub)
    out = pl.pallas_call(
        kern,
        grid=(d, T // rows),
        in_specs=in_specs,
        out_specs=out_specs,
        out_shape=out_shape,
        scratch_shapes=[pltpu.VMEM((BAND + rows, wk), BF16), pltpu.VMEM((BAND + rows, wk), BF16)],
        compiler_params=_params(("parallel", "parallel")),
        name=f"band_d{d}" + ("_sink" if sinks is not None else ""),
    )(*args)
    return out if with_lse else out[0]


def _odd_proj_kernel(h_ref, gn_ref, wq_ref, wk_ref, wv_ref, q_ref, k_ref, v_ref, *, q_scale):
    hn = _rms(h_ref[...], gn_ref[...]).astype(BF16)
    q_ref[...] = (_dot(hn, wq_ref[...]) * q_scale).astype(BF16)
    k_ref[...] = _dot(hn, wk_ref[...]).astype(BF16)
    v_ref[...] = _dot(hn, wv_ref[...]).astype(BF16)


def _odd_proj(h, gn, wq, wk, wv):
    S = h.shape[0]
    tm = ROW_TILE
    row = lambda w: pl.BlockSpec((tm, w), lambda i: (i, 0))
    kern = functools.partial(_odd_proj_kernel, q_scale=1.0 / math.sqrt(SWA_HEAD_DIM))
    return pl.pallas_call(
        kern,
        grid=(S // tm,),
        in_specs=[row(D_MODEL), _resident(gn.shape), _resident(wq.shape), _resident(wk.shape),
                  _resident(wv.shape)],
        out_specs=[row(wq.shape[1]), row(wk.shape[1]), row(wv.shape[1])],
        out_shape=[jax.ShapeDtypeStruct((S, wq.shape[1]), BF16),
                   jax.ShapeDtypeStruct((S, wk.shape[1]), BF16),
                   jax.ShapeDtypeStruct((S, wv.shape[1]), BF16)],
        compiler_params=_params(("parallel",)),
        name="odd_proj",
    )(h, gn, wq, wk, wv)


def _merge_patterns(pat_refs, expand_ref, nat_refs, dils, tm):
    outs, lses = [], []
    nat = iter(nat_refs)
    for n, d in enumerate(dils):
        o_ref, l_ref = pat_refs[2 * n], pat_refs[2 * n + 1]
        if d == 1:
            outs.append(o_ref[0].astype(F32))
            lses.append(l_ref[0])
        else:
            o_nat, l_nat = next(nat), next(nat)
            for r in range(d):
                rows = pl.ds(r, tm // d, stride=d)
                block = o_ref[r].astype(F32)
                for g in range(o_nat.shape[0]):
                    o_nat[g, rows, :] = block[:, LANES * g:LANES * (g + 1)]
                l_nat[rows, :] = l_ref[r]
            outs.append(jnp.concatenate([o_nat[g] for g in range(o_nat.shape[0])], axis=1))
            lses.append(l_nat[...])
    top = functools.reduce(jnp.maximum, lses)
    es = [jnp.exp(l - top) for l in lses]
    inv = 1.0 / functools.reduce(jnp.add, es)
    merged = None
    for o, e in zip(outs, es):
        w = e * inv
        w_hi = w.astype(BF16)
        w_lo = (w - w_hi.astype(F32)).astype(BF16)
        w_full = _dot(w_hi, expand_ref[...]) + _dot(w_lo, expand_ref[...])
        merged = o * w_full if merged is None else merged + o * w_full
    return merged.astype(BF16)


def _post_kernel(*refs, final_norm, dils):
    refs = list(refs)
    h_ref, a1_ref = refs[:2]
    refs = refs[2:]
    tm = h_ref.shape[0]
    if dils:
        pat_refs, expand_ref = refs[:2 * len(dils)], refs[2 * len(dils)]
        refs = refs[2 * len(dils) + 1:]
    else:
        a2_ref = refs.pop(0)
    wo1_ref, wo2_ref, gf_ref, wgu_ref, wd_ref = refs[:5]
    refs = refs[5:]
    gfin_ref = refs.pop(0) if final_norm else None
    o_ref = refs.pop(0)
    a2 = _merge_patterns(pat_refs, expand_ref, refs, dils, tm) if dils else a2_ref[...]

    h1 = h_ref[...] + _dot(a1_ref[...], wo1_ref[...]) + _dot(a2, wo2_ref[...])
    hn = _rms(h1, gf_ref[...]).astype(BF16)
    gate = _dot(hn, wgu_ref[:, :D_FF])
    up = _dot(hn, wgu_ref[:, D_FF:])
    act = (gate / (1.0 + jnp.exp(-gate)) * up).astype(BF16)
    out = h1 + _dot(act, wd_ref[...])
    if final_norm:
        out = _rms(out, gfin_ref[...])
    o_ref[...] = out


def _post(h, a1, a2, wo1, wo2, gf, wgu, wd, gfin=None):
    S = h.shape[0]
    tm = POST_TILE
    w_half = wo1.shape[0]
    row = lambda w, c=0: pl.BlockSpec((tm, w), lambda i, c=c: (i, c))
    args, in_specs, scratch, dils = [h, a1], [row(D_MODEL), row(w_half, 0)], [], ()
    if isinstance(a2, list):
        dils = tuple(d for d, _, _ in a2)
        for d, o, lse in a2:
            args += [o, lse]
            in_specs += [pl.BlockSpec((d, tm // d, w_half), lambda i: (0, i, 0)),
                         pl.BlockSpec((d, tm // d, LANES), lambda i: (0, i, 0))]
            if d > 1:
                scratch += [pltpu.VMEM((w_half // LANES, tm, LANES), F32), pltpu.VMEM((tm, LANES), F32)]
        head_of_lane = jnp.arange(w_half) // DIL_HEAD_DIM
        expand = (jnp.arange(LANES)[:, None] == head_of_lane[None, :]).astype(BF16)
        args.append(expand)
        in_specs.append(_resident(expand.shape))
    else:
        args.append(a2)
        in_specs.append(row(w_half, 1 if a2 is a1 else 0))
    args += [wo1, wo2, gf, wgu, wd]
    in_specs += [_resident(wo1.shape), _resident(wo2.shape), _resident(gf.shape),
                 _resident(wgu.shape), _resident(wd.shape)]
    if gfin is not None:
        args.append(gfin)
        in_specs.append(_resident(gfin.shape))
    return pl.pallas_call(
        functools.partial(_post_kernel, final_norm=gfin is not None, dils=dils),
        grid=(S // tm,),
        in_specs=in_specs,
        out_specs=row(D_MODEL),
        out_shape=jax.ShapeDtypeStruct((S, D_MODEL), F32),
        scratch_shapes=scratch,
        compiler_params=_params(("parallel",)),
        name=("post_merge" if dils else "post") + ("_final" if gfin is not None else ""),
    )(*args)


def _alibi_slopes(n):
    return [2.0 ** (-8.0 * (i + 1) / n) for i in range(n)]


def _rope_tables(S):
    pos = jnp.arange(S, dtype=F32)
    inv_freq = ROPE_BASE ** (-jnp.arange(0, MLA_ROPE, 2, dtype=F32) / MLA_ROPE)
    ang = pos[:, None] * inv_freq[None, :]
    cos, sin = jnp.cos(ang), jnp.sin(ang)
    ones = jnp.ones((S, MLA_NOPE), F32)
    zeros = jnp.zeros((S, MLA_NOPE), F32)
    pad = jnp.zeros((S, LANES - MLA_QK), F32)
    return (jnp.concatenate([ones, cos, cos, pad], axis=1),
            jnp.concatenate([zeros, -sin, sin, pad], axis=1))


def _pad_heads(w, heads, width):
    rows = w.shape[0]
    w = w.reshape(rows, heads, width)
    return jnp.pad(w, ((0, 0), (0, 0), (0, LANES - width))).reshape(rows, heads * LANES)


def _even_weights(w_in, w_uq, w_ukv):
    o2 = MLA_Q_LORA + MLA_KV_LORA
    o3 = o2 + MLA_ROPE
    z = lambda n: jnp.zeros((D_MODEL, n), F32)
    wa = jnp.concatenate([w_in[:, :o2], z(MLA_NOPE), w_in[:, o2:o3], z(LANES - MLA_QK)], axis=1)
    wb = w_in[:, o3:]
    wuq = _pad_heads(w_uq, MLA_HEADS, MLA_QK)
    wukv = w_ukv.reshape(MLA_KV_LORA, MLA_HEADS, MLA_NOPE + MLA_V)
    wuk = _pad_heads(wukv[..., :MLA_NOPE].reshape(MLA_KV_LORA, -1), MLA_HEADS, MLA_NOPE)
    wuvt = wukv[..., MLA_NOPE:].reshape(MLA_KV_LORA, -1).T
    return [w.astype(BF16) for w in (wa, wb, wuq, wuk, wuvt)]


def _odd_weights(w_qkv):
    nq = SWA_Q_HEADS * SWA_HEAD_DIM
    nk = SWA_KV_HEADS * SWA_HEAD_DIM
    dup = lambda w: jnp.repeat(w.reshape(D_MODEL, SWA_KV_HEADS, 1, SWA_HEAD_DIM), 2, axis=2).reshape(D_MODEL, -1)
    return [w.astype(BF16) for w in (w_qkv[:, :nq], dup(w_qkv[:, nq:nq + nk]), dup(w_qkv[:, nq + nk:]))]


def kernel(x, attn_norm, ffn_norm, final_norm, e_w_in, e_q_norm, e_kv_norm, e_w_uq, e_w_ukv, e_w_out,
           o_w_qkv, o_sinks, o_w_out, f_w_gate_up, f_w_down):
    B, S, D = x.shape
    dils = tuple(d for _, d in DIL_PATTERNS)
    assert B == 1 and D == D_MODEL and S % POST_TILE == 0 and S % (BAND * max(dils)) == 0
    h = x.reshape(S, D)
    ct, st = _rope_tables(S)
    dil_slopes = _alibi_slopes(DIL_HEADS)
    swa_slopes = _alibi_slopes(SWA_Q_HEADS)
    row = lambda g: g.reshape(1, -1)

    for layer in range(DEPTH):
        i = layer // 2
        if layer % 2 == 0:
            wa, wb, wuq, wuk, wuvt = _even_weights(e_w_in[i], e_w_uq[i], e_w_ukv[i])
            (q, k, vt), dilated = _even_proj(h, row(attn_norm[layer]), wa, wb, row(e_q_norm[i]),
                                             row(e_kv_norm[i]), wuq, wuk, wuvt, ct, st, dils)
            a1 = _mla_attention(q, k, vt)
            a2 = []
            for (window, d), (qd, kd, vd) in zip(DIL_PATTERNS, dilated):
                o, lse = _band_attention(qd, kd, vd, slopes=[s * d for s in dil_slopes],
                                         max_dist=window // d, with_lse=True)
                a2.append((d, o, lse))
            w_out = e_w_out[i].astype(BF16)
            half = MLA_HEADS * MLA_V
        else:
            wq, wk, wv = _odd_weights(o_w_qkv[i])
            q, k, v = _odd_proj(h, row(attn_norm[layer]), wq, wk, wv)
            a1 = a2 = _band_attention(q[None], k[None], v[None], slopes=swa_slopes,
                                      max_dist=SWA_WINDOW - 1, sinks=o_sinks[i])[0]
            w_out = o_w_out[i].astype(BF16)
            half = SWA_Q_HEADS * SWA_HEAD_DIM // 2
        h = _post(h, a1, a2, w_out[:half], w_out[half:], row(ffn_norm[layer]),
                  f_w_gate_up[layer].astype(BF16), f_w_down[layer].astype(BF16),
                  row(final_norm) if layer == DEPTH - 1 else None)
    return h.reshape(B, S, D)
```
